```python
import jax, jax.numpy as jnp
from jax import lax
import numpy as np

D_MODEL = 1024
BATCH = 16
SEQ = 2048
DEPTH = 2
DEC_BATCH = 32
DEC_SEQ = 32
PAST_LEN = 1024

CHUNK = 64
MIX_WIDTH = D_MODEL
N_SB_HEADS = 8
SB_HEAD_DIM = 64
SB_WIDTH = N_SB_HEADS * SB_HEAD_DIM
SB_QBLOCK = 128
N_A_GROUPS = 8
A_GROUP_DIM = 64
A_WIDTH = N_A_GROUPS * A_GROUP_DIM
GMLP_CHUNK = 128
IN_WIDTH = 3 * SB_WIDTH + 2 * A_WIDTH
PEER_HEADS = 8
PEER_NKEYS = 128
PEER_EXPERTS = PEER_NKEYS * PEER_NKEYS
PEER_TOPK = 16
PEER_DKEY = 128
PEER_DKEY_HALF = PEER_DKEY // 2
PEER_BLOCK = 256
PLE_DIM = 256
RMS_EPS = 1e-6

kernel_name = "stickbreak_chunkmlp_peer_streaming_step"


def rmsnorm(x, g):
    xf = x.astype(jnp.float32)
    y = xf * lax.rsqrt(jnp.mean(xf * xf, axis=-1, keepdims=True) + RMS_EPS)
    return (y * g.astype(jnp.float32)).astype(x.dtype)


def stick_breaking_block(q, k, v, q_pos, k_pos):
    z = jnp.einsum('bqhd,bkhd->bhqk', q, k).astype(jnp.float32)
    mask = k_pos[None, :] < q_pos[:, None]
    log_beta = jax.nn.log_sigmoid(z)
    log_1m = jnp.where(mask, jax.nn.log_sigmoid(-z), 0.0)
    rest = lax.cumsum(log_1m, axis=3, reverse=True) - log_1m
    w = jnp.where(mask, jnp.exp(log_beta + rest), 0.0)
    return jnp.einsum('bhqk,bkhd->bqhd', w.astype(v.dtype), v)


def spatial_gate(u, v, w_s, b_s):
    B, T, G, Dg = v.shape
    L = min(T, GMLP_CHUNK)
    mask = jnp.tril(jnp.ones((L, L), dtype=bool))
    w = jnp.where(mask[None], w_s[:, :L, :L], 0.0)
    vc = v.reshape(B, T // L, L, G, Dg)
    mix = jnp.einsum('gts,bcsgd->bctgd', w, vc) + jnp.transpose(b_s[:, :L])[None, None, :, :, None]
    return u * mix.reshape(B, T, G, Dg)


def token_mixers(xn, past_k, past_v, w_in, g_sb_out, w_s, b_s, g_a_v, g_a_out, w_out):
    B, T, _ = xn.shape
    proj = xn @ w_in
    q, k, v, u_a, v_a = jnp.split(
        proj, [SB_WIDTH, 2 * SB_WIDTH, 3 * SB_WIDTH, 3 * SB_WIDTH + A_WIDTH], axis=-1)
    q = q.reshape(B, T, N_SB_HEADS, SB_HEAD_DIM) * (SB_HEAD_DIM ** -0.5)
    k = k.reshape(B, T, N_SB_HEADS, SB_HEAD_DIM)
    v = v.reshape(B, T, N_SB_HEADS, SB_HEAD_DIM)
    if past_k is None:
        k_all, v_all, P = k, v, 0
    else:
        k_all = jnp.concatenate([past_k.astype(k.dtype), k], axis=1)
        v_all = jnp.concatenate([past_v.astype(v.dtype), v], axis=1)
        P = past_k.shape[1]
    k_pos = jnp.arange(P + T)
    q_pos = P + jnp.arange(T)
    outs = []
    for qs in range(0, T, SB_QBLOCK):
        qe = min(qs + SB_QBLOCK, T)
        outs.append(stick_breaking_block(q[:, qs:qe], k_all[:, :P + qe], v_all[:, :P + qe],
                                         q_pos[qs:qe], k_pos[:P + qe]))
    o_sb = rmsnorm(jnp.concatenate(outs, axis=1), g_sb_out)
    u_a = jax.nn.gelu(u_a, approximate=False).reshape(B, T, N_A_GROUPS, A_GROUP_DIM)
    v_a = rmsnorm(jax.nn.gelu(v_a, approximate=False).reshape(B, T, N_A_GROUPS, A_GROUP_DIM), g_a_v)
    o_a = rmsnorm(spatial_gate(u_a, v_a, w_s, b_s), g_a_out)
    o = jnp.concatenate([o_sb.reshape(B, T, SB_WIDTH), o_a.reshape(B, T, A_WIDTH)], axis=-1) @ w_out
    return o, k, v, v_a


def peer_tokens(xb, w_q, sub_keys, u_tab, v_tab):
    t = xb.shape[0]
    q = (xb @ w_q).reshape(t, PEER_HEADS, 2, PEER_DKEY_HALF)
    s = jnp.einsum('thpd,hpnd->thpn', q, sub_keys).astype(jnp.float32)
    s1, i1 = lax.top_k(s[:, :, 0], PEER_TOPK)
    s2, i2 = lax.top_k(s[:, :, 1], PEER_TOPK)
    cand = (s1[..., :, None] + s2[..., None, :]).reshape(t, PEER_HEADS, PEER_TOPK * PEER_TOPK)
    cidx = (i1[..., :, None] * PEER_NKEYS + i2[..., None, :]).reshape(t, PEER_HEADS, PEER_TOPK * PEER_TOPK)
    sc, sel = lax.top_k(cand, PEER_TOPK)
    idx = jnp.take_along_axis(cidx, sel, axis=-1)
    g = jax.nn.softmax(sc, axis=-1)
    u_sel = jnp.take(u_tab, idx, axis=0)
    hid = jnp.einsum('td,thkd->thk', xb, u_sel).astype(jnp.float32)
    a = (g * jax.nn.gelu(hid, approximate=False)).astype(xb.dtype)
    return jnp.einsum('thk,thkd->td', a, jnp.take(v_tab, idx, axis=0))


def peer(x, w_q, sub_keys, u_tab, v_tab):
    B, T, D = x.shape
    n = B * T
    blk = min(PEER_BLOCK, n)
    nb = -(-n // blk)
    xf = jnp.pad(x.reshape(n, D), ((0, nb * blk - n), (0, 0))).reshape(nb, blk, D)
    out = lax.map(lambda xb: peer_tokens(xb, w_q, sub_keys, u_tab, v_tab), xf)
    return out.reshape(nb * blk, D)[:n].reshape(B, T, D)


def trunk_layer(h, p_l, past_k, past_v, g_mix, w_in, g_sb_out, w_s, b_s, g_a_v, g_a_out, w_out,
                g_ffn, w_peer_q, sub_keys, u_tab, v_tab, g_ple, w_ple_gate, b_ple_gate, w_ple):
    mix, k, v, v_a = token_mixers(rmsnorm(h, g_mix), past_k, past_v, w_in, g_sb_out,
                                  w_s, b_s, g_a_v, g_a_out, w_out)
    h = h + mix
    h = h + peer(rmsnorm(h, g_ffn), w_peer_q, sub_keys, u_tab, v_tab)
    gate = jax.nn.sigmoid(rmsnorm(h, g_ple) @ w_ple_gate + b_ple_gate)
    h = h + (p_l @ w_ple) * gate
    return h, k, v, v_a


def setup_inputs(seed: int = 0) -> dict:
    key = jax.random.key(seed)
    ks = jax.random.split(key, 26)
    nrm = lambda k, shape, scale: jax.random.normal(k, shape, jnp.float32) * scale
    gain = lambda k, shape: 1.0 + 0.05 * jax.random.normal(k, shape, jnp.float32)
    D = D_MODEL
    return {
        "x_prompt": nrm(ks[0], (BATCH, SEQ, D), 1.0),
        "x_sample": nrm(ks[1], (DEC_BATCH, DEC_SEQ, D), 1.0),
        "cache_k": nrm(ks[2], (DEPTH, DEC_BATCH, PAST_LEN, N_SB_HEADS, SB_HEAD_DIM), 1.0),
        "cache_v": nrm(ks[3], (DEPTH, DEC_BATCH, PAST_LEN, N_SB_HEADS, SB_HEAD_DIM), 1.0),
        "p_prompt": nrm(ks[4], (DEPTH, BATCH, SEQ, PLE_DIM), 1.0),
        "p_sample": nrm(ks[5], (DEPTH, DEC_BATCH, DEC_SEQ, PLE_DIM), 1.0),
        "g_mix": gain(ks[6], (DEPTH, D)),
        "w_in": nrm(ks[7], (DEPTH, D, IN_WIDTH), D ** -0.5),
        "g_sb_out": gain(ks[8], (DEPTH, N_SB_HEADS, SB_HEAD_DIM)),
        "w_spatial": nrm(ks[9], (DEPTH, N_A_GROUPS, GMLP_CHUNK, GMLP_CHUNK), GMLP_CHUNK ** -0.5),
        "b_spatial": gain(ks[10], (DEPTH, N_A_GROUPS, GMLP_CHUNK)),
        "g_a_v": gain(ks[11], (DEPTH, N_A_GROUPS, A_GROUP_DIM)),
        "g_a_out": gain(ks[12], (DEPTH, N_A_GROUPS, A_GROUP_DIM)),
        "w_out": nrm(ks[13], (DEPTH, MIX_WIDTH, D), MIX_WIDTH ** -0.5),
        "g_ffn": gain(ks[14], (DEPTH, D)),
        "w_peer_q": nrm(ks[15], (DEPTH, D, PEER_HEADS * PEER_DKEY), D ** -0.5),
        "peer_sub_keys": nrm(ks[16], (DEPTH, PEER_HEADS, 2, PEER_NKEYS, PEER_DKEY_HALF), PEER_DKEY_HALF ** -0.5),
        "peer_u": nrm(ks[17], (DEPTH, PEER_EXPERTS, D), D ** -0.5),
        "peer_v": nrm(ks[18], (DEPTH, PEER_EXPERTS, D), 0.5),
        "g_ple": gain(ks[19], (DEPTH, D)),
        "w_ple_gate": nrm(ks[20], (DEPTH, D, D), D ** -0.5),
        "b_ple_gate": nrm(ks[21], (DEPTH, D), 0.02),
        "w_ple": nrm(ks[22], (DEPTH, PLE_DIM, D), PLE_DIM ** -0.5),
        "g_final": gain(ks[23], (D,)),
    }


def reference(x_prompt, x_sample, cache_k, cache_v, p_prompt, p_sample, g_mix, w_in, g_sb_out,
              w_spatial, b_spatial, g_a_v, g_a_out, w_out, g_ffn, w_peer_q, peer_sub_keys,
              peer_u, peer_v, g_ple, w_ple_gate, b_ple_gate, w_ple, g_final):
    h_p, h_s = x_prompt, x_sample
    kp_l, vp_l, ks_l, vs_l, vas_l = [], [], [], [], []
    for l in range(DEPTH):
        lp = (g_mix[l], w_in[l], g_sb_out[l], w_spatial[l], b_spatial[l], g_a_v[l], g_a_out[l],
              w_out[l], g_ffn[l], w_peer_q[l], peer_sub_keys[l], peer_u[l], peer_v[l],
              g_ple[l], w_ple_gate[l], b_ple_gate[l], w_ple[l])
        h_p, kp, vp, _ = trunk_layer(h_p, p_prompt[l], None, None, *lp)
        h_s, ks, vs, vas = trunk_layer(h_s, p_sample[l], cache_k[l], cache_v[l], *lp)
        kp_l.append(kp); vp_l.append(vp); ks_l.append(ks); vs_l.append(vs); vas_l.append(vas)
    y_prompt = rmsnorm(h_p, g_final)
    y_sample = rmsnorm(h_s, g_final)
    new_k_prompt = jnp.stack(kp_l)
    new_v_prompt = jnp.stack(vp_l)
    new_k_sample = jnp.stack(ks_l)
    new_v_sample = jnp.stack(vs_l)
    new_chunkmlp_v_sample = jnp.stack(vas_l)
    return (y_prompt, y_sample, new_k_prompt, new_v_prompt, new_k_sample, new_v_sample, new_chunkmlp_v_sample)
```

```python
import functools

import jax
import jax.numpy as jnp
from jax import lax
from jax.experimental import pallas as pl
from jax.experimental.pallas import tpu as pltpu

F32 = jnp.float32
BF16 = jnp.bfloat16

LANES = 128
D_MODEL = 1024
N_HEADS = 8
HEAD_DIM = 64
SB_W = N_HEADS * HEAD_DIM
A_W = 512
IN_W = 3 * SB_W + 2 * A_W
GROUP = 64
MIX_L = 128
PEER_HEADS = 8
PEER_NKEYS = 128
PEER_TOPK = 16
PEER_EXPERTS = PEER_NKEYS * PEER_NKEYS
PLE_DIM = 256
RMS_EPS = 1e-6
TOK_TILE = 256
SB_BLOCK = 256
PEER_CHUNK = 1024
GS_PAD = 8
VMEM_LIMIT = 48 * 1024 * 1024
INV_SQRT2 = 0.7071067811865476


def _dot(a, b):
    return jnp.dot(a, b, preferred_element_type=F32)


def _dot_nt(a, b):
    return lax.dot_general(a, b, (((1,), (1,)), ((), ())), preferred_element_type=F32)


def _split_bf16(x):
    hi = x.astype(BF16)
    lo = (x - hi.astype(F32)).astype(BF16)
    return hi, lo


def _dot_f32lhs(x, w):
    hi, lo = _split_bf16(x)
    return _dot(hi, w) + _dot(lo, w)


def _gelu(x):
    return 0.5 * x * (1.0 + lax.erf(x * INV_SQRT2))


def _rms(x, g):
    ms = jnp.mean(x * x, axis=-1, keepdims=True)
    return x * lax.rsqrt(ms + RMS_EPS) * g


def _group_mean_matrix():
    r = lax.broadcasted_iota(jnp.int32, (LANES, LANES), 0) // GROUP
    c = lax.broadcasted_iota(jnp.int32, (LANES, LANES), 1) // GROUP
    return jnp.where(r == c, 1.0 / GROUP, 0.0).astype(BF16)


def _group_rms(x, g, bd):
    xx = x * x
    hi, lo = _split_bf16(xx)
    parts = []
    for j in range(x.shape[1] // LANES):
        sl = slice(LANES * j, LANES * (j + 1))
        parts.append(_dot(hi[:, sl], bd) + _dot(lo[:, sl], bd))
    ms = jnp.concatenate(parts, axis=1)
    return x * lax.rsqrt(ms + RMS_EPS) * g


def _inproj_kernel(x_ref, g_ref, w_ref, q_ref, kf_ref, vf_ref, kb_ref, vb_ref, ua_ref, va_ref):
    xn = _rms(x_ref[...], g_ref[...]).astype(BF16)
    q = _dot(xn, w_ref[:, 0:SB_W])
    q_ref[...] = (q * (HEAD_DIM ** -0.5)).astype(BF16)
    k = _dot(xn, w_ref[:, SB_W:2 * SB_W])
    kf_ref[...] = k
    kb_ref[...] = k.astype(BF16)
    v = _dot(xn, w_ref[:, 2 * SB_W:3 * SB_W])
    vf_ref[...] = v
    vb_ref[...] = v.astype(BF16)
    ua_ref[...] = _dot(xn, w_ref[:, 3 * SB_W:3 * SB_W + A_W])
    va_ref[...] = _dot(xn, w_ref[:, 3 * SB_W + A_W:IN_W])


def _inproj(h, g, w):
    n = h.shape[0]
    tm = TOK_TILE
    row = lambda i: (i, 0)
    fixed = lambda i: (0, 0)
    out_f = jax.ShapeDtypeStruct((n, SB_W), F32)
    out_b = jax.ShapeDtypeStruct((n, SB_W), BF16)
    blk = pl.BlockSpec((tm, SB_W), row)
    return pl.pallas_call(
        _inproj_kernel,
        grid=(n // tm,),
        in_specs=[pl.BlockSpec((tm, D_MODEL), row), pl.BlockSpec((1, D_MODEL), fixed),
                  pl.BlockSpec((D_MODEL, IN_W), fixed)],
        out_specs=[blk] * 7,
        out_shape=[out_b, out_f, out_f, out_b, out_b, out_f, out_f],
        compiler_params=pltpu.CompilerParams(dimension_semantics=("parallel",),
                                             vmem_limit_bytes=VMEM_LIMIT),
        name="inproj",
    )(h, g, w)


def _strict_lower_ones(n):
    r = lax.broadcasted_iota(jnp.int32, (n, n), 0)
    c = lax.broadcasted_iota(jnp.int32, (n, n), 1)
    return jnp.where(r > c, 1.0, 0.0).astype(BF16)


def _sb_block(qh, k2, v2h, carry, ustrict, ones, mask):
    tk = k2.shape[0]
    z = _dot_nt(qh, k2)
    sp = jnp.log1p(jnp.exp(-jnp.abs(z)))
    log_beta = jnp.minimum(z, 0.0) - sp
    log_1m = log_beta - z
    if mask is not None:
        log_1m = jnp.where(mask, log_1m, 0.0)
    hi, lo = _split_bf16(log_1m)
    later = _dot(hi, ustrict) + _dot(lo, ustrict)
    rowsum = _dot(hi, ones) + _dot(lo, ones)
    rest = later + jnp.concatenate([carry] * (tk // LANES), axis=1)
    w = jnp.exp(log_beta + rest)
    if mask is not None:
        w = jnp.where(mask, w, 0.0)
    return _dot(w.astype(BF16), v2h), carry + rowsum


def _sb_prompt_kernel(q_ref, k_ref, v_ref, o_ref, acc_ref, carry_ref):
    tq = q_ref.shape[0]
    tk = tq
    qi = pl.program_id(2)
    ustrict = _strict_lower_ones(tk)
    ones = jnp.ones((tk, LANES), BF16)
    r = lax.broadcasted_iota(jnp.int32, (tq, tk), 0)
    c = lax.broadcasted_iota(jnp.int32, (tq, tk), 1)
    diag = c < r
    lane_head = lax.broadcasted_iota(jnp.int32, (1, LANES), 1) // HEAD_DIM
    q2 = q_ref[...]
    acc_ref[...] = jnp.zeros_like(acc_ref)
    for hh in range(2):
        hm = lane_head == hh
        qh = jnp.where(hm, q2, jnp.zeros_like(q2))
        carry_ref[...] = jnp.zeros_like(carry_ref)

        def step(kb, mask, qh=qh, hm=hm):
            start = pl.multiple_of(kb * tk, tk)
            k2 = k_ref[pl.ds(start, tk), :]
            v2 = v_ref[pl.ds(start, tk), :]
            v2h = jnp.where(hm, v2, jnp.zeros_like(v2))
            out, carry = _sb_block(qh, k2, v2h, carry_ref[...], ustrict, ones, mask)
            acc_ref[...] += out
            carry_ref[...] = carry

        step(qi, diag)

        def body(it, _, step=step):
            step(qi - 1 - it, None)
            return 0

        lax.fori_loop(0, qi, body, 0)
    o_ref[...] = acc_ref[...]


def _sb_prompt(q, kb, vb, batch, seq):
    tq = SB_BLOCK
    nq = seq // tq
    pairs = SB_W // LANES
    return pl.pallas_call(
        _sb_prompt_kernel,
        grid=(batch, pairs, nq),
        in_specs=[pl.BlockSpec((tq, LANES), lambda b, j, i: (b * nq + i, j)),
                  pl.BlockSpec((seq, LANES), lambda b, j, i: (b, j)),
                  pl.BlockSpec((seq, LANES), lambda b, j, i: (b, j))],
        out_specs=pl.BlockSpec((tq, LANES), lambda b, j, i: (b * nq + i, j)),
        out_shape=jax.ShapeDtypeStruct((batch * seq, SB_W), F32),
        scratch_shapes=[pltpu.VMEM((tq, LANES), F32), pltpu.VMEM((tq, LANES), F32)],
        compiler_params=pltpu.CompilerParams(
            dimension_semantics=("parallel", "parallel", "arbitrary"), vmem_limit_bytes=VMEM_LIMIT),
        name="sb_prompt",
    )(q, kb, vb)


def _sb_sample_kernel(q_ref, k_ref, v_ref, ck_ref, cv_ref, o_ref):
    ts = q_ref.shape[0]
    past = ck_ref.shape[0]
    tk = SB_BLOCK
    ustrict = _strict_lower_ones(tk)
    ones = jnp.ones((tk, LANES), BF16)
    ustrict_d = ustrict[:LANES, :LANES]
    ones_d = ones[:LANES, :]
    r = lax.broadcasted_iota(jnp.int32, (ts, LANES), 0)
    c = lax.broadcasted_iota(jnp.int32, (ts, LANES), 1)
    diag = c < r
    lane_head = lax.broadcasted_iota(jnp.int32, (1, LANES), 1) // HEAD_DIM
    q2 = q_ref[...]
    pad = jnp.zeros((LANES - ts, LANES), BF16)
    kn = jnp.concatenate([k_ref[...], pad], axis=0)
    vn = jnp.concatenate([v_ref[...], pad], axis=0)
    acc = jnp.zeros((ts, LANES), F32)
    for hh in range(2):
        hm = lane_head == hh
        qh = jnp.where(hm, q2, jnp.zeros_like(q2))
        carry = jnp.zeros((ts, LANES), F32)
        out, carry = _sb_block(qh, kn, jnp.where(hm, vn, jnp.zeros_like(vn)), carry,
                               ustrict_d, ones_d, diag)
        acc = acc + out
        for kb in range(past // tk - 1, -1, -1):
            k2 = ck_ref[kb * tk:(kb + 1) * tk, :].astype(BF16)
            v2 = cv_ref[kb * tk:(kb + 1) * tk, :].astype(BF16)
            out, carry = _sb_block(qh, k2, jnp.where(hm, v2, jnp.zeros_like(v2)), carry,
                                   ustrict, ones, None)
            acc = acc + out
    o_ref[...] = acc


def _sb_sample(q, kb, vb, cache_k, cache_v, n_prompt, batch, ts):
    past = cache_k.shape[1]
    pairs = SB_W // LANES
    off = n_prompt // ts
    new = pl.BlockSpec((ts, LANES), lambda b, j: (off + b, j))
    old = pl.BlockSpec((None, past, LANES), lambda b, j: (b, 0, j))
    return pl.pallas_call(
        _sb_sample_kernel,
        grid=(batch, pairs),
        in_specs=[new, new, new, old, old],
        out_specs=pl.BlockSpec((ts, LANES), lambda b, j: (b, j)),
        out_shape=jax.ShapeDtypeStruct((batch * ts, SB_W), F32),
        compiler_params=pltpu.CompilerParams(dimension_semantics=("parallel", "parallel"),
                                             vmem_limit_bytes=VMEM_LIMIT),
        name="sb_sample",
    )(q, kb, vb, cache_k, cache_v)


def _mixer_out_kernel(h_ref, ua_ref, va_ref, o_ref, ws_ref, bs_ref, gav_ref, gao_ref, gsb_ref, wo_ref,
                      h1_ref, vn_ref):
    tm = h_ref.shape[0]
    bd = _group_mean_matrix()
    lane_group = lax.broadcasted_iota(jnp.int32, (1, LANES), 1) // GROUP
    u = _gelu(ua_ref[...])
    vn = _group_rms(_gelu(va_ref[...]), gav_ref[...], bd)
    vn_ref[...] = vn
    vnb = vn.astype(BF16)
    rows = []
    for rr in range(tm // MIX_L):
        slabs = []
        for j in range(A_W // LANES):
            v2 = vnb[rr * MIX_L:(rr + 1) * MIX_L, j * LANES:(j + 1) * LANES]
            mix = jnp.zeros((MIX_L, LANES), F32)
            for gg in range(2):
                v2g = jnp.where(lane_group == gg, v2, jnp.zeros_like(v2))
                mix = mix + _dot(ws_ref[2 * j + gg], v2g)
            slabs.append(mix)
        rows.append(jnp.concatenate(slabs, axis=1) + bs_ref[...])
    mix = jnp.concatenate(rows, axis=0)
    oa = _group_rms(u * mix, gao_ref[...], bd)
    osb = _group_rms(o_ref[...], gsb_ref[...], bd)
    cat = jnp.concatenate([osb, oa], axis=1).astype(BF16)
    h1_ref[...] = h_ref[...] + _dot(cat, wo_ref[...])


def _mixer_out(h, ua, va, o, ws, bs, gav, gao, gsb, wo, n_prompt):
    n = h.shape[0]
    tm = TOK_TILE
    npt = n_prompt // tm
    row = lambda i: (i, 0)
    fixed = lambda i: (0, 0)
    variant = lambda i: jnp.where(i < npt, 0, 1)
    return pl.pallas_call(
        _mixer_out_kernel,
        grid=(n // tm,),
        in_specs=[pl.BlockSpec((tm, D_MODEL), row), pl.BlockSpec((tm, A_W), row),
                  pl.BlockSpec((tm, A_W), row), pl.BlockSpec((tm, SB_W), row),
                  pl.BlockSpec((None, 2 * A_W // LANES, MIX_L, MIX_L), lambda i: (variant(i), 0, 0, 0)),
                  pl.BlockSpec((None, MIX_L, A_W), lambda i: (variant(i), 0, 0)),
                  pl.BlockSpec((1, A_W), fixed), pl.BlockSpec((1, A_W), fixed),
                  pl.BlockSpec((1, SB_W), fixed), pl.BlockSpec((D_MODEL, D_MODEL), fixed)],
        out_specs=[pl.BlockSpec((tm, D_MODEL), row), pl.BlockSpec((tm, A_W), row)],
        out_shape=[jax.ShapeDtypeStruct((n, D_MODEL), F32), jax.ShapeDtypeStruct((n, A_W), F32)],
        compiler_params=pltpu.CompilerParams(dimension_semantics=("parallel",),
                                             vmem_limit_bytes=VMEM_LIMIT),
        name="mixer_out",
    )(h, ua, va, o, ws, bs, gav, gao, gsb, wo)


def _spatial_params(w_s, b_s, ts):
    tril = jnp.tril(jnp.ones((MIX_L, MIX_L), bool))
    w_p = jnp.where(tril[None], w_s, 0.0)
    rep = MIX_L // ts
    w_small = jnp.where(tril[None, :ts, :ts], w_s[:, :ts, :ts], 0.0)
    eye = jnp.eye(rep, dtype=w_s.dtype)
    w_smp = jnp.einsum("ab,gts->gatbs", eye, w_small).reshape(-1, MIX_L, MIX_L)
    b_p = jnp.repeat(jnp.transpose(b_s), A_W // b_s.shape[0], axis=1)
    b_smp = jnp.tile(b_p[:ts], (rep, 1))
    return jnp.stack([w_p, w_smp]).astype(BF16), jnp.stack([b_p, b_smp])


def _peer_route_kernel(h_ref, g_ref, wq_ref, km_ref, xn_ref, e_ref, gate_ref):
    tm = h_ref.shape[0]
    k_top = PEER_TOPK
    xn = _rms(h_ref[...], g_ref[...]).astype(BF16)
    xn_ref[...] = xn
    lane_i = lax.broadcasted_iota(jnp.int32, (tm, LANES), 1)
    lane_f = lane_i.astype(F32)
    lane2_i = lax.broadcasted_iota(jnp.int32, (tm, 2 * LANES), 1)
    lane2_hi = lane2_i // k_top
    lane2_lo = lane2_i % k_top
    pos_code = lane2_i.astype(F32) * float(PEER_EXPERTS)
    neg_inf = -jnp.inf
    big = float(1 << 23)
    e_ref[...] = jnp.zeros_like(e_ref)
    gate_ref[...] = jnp.zeros_like(gate_ref)

    def head(hd, _):
        pq = _dot(xn, wq_ref[hd])
        s = _dot(pq.astype(BF16), km_ref[hd])
        sa = s[:, :LANES]
        sb = s[:, LANES:]
        s1 = jnp.zeros((tm, 2 * LANES), F32)
        s2 = jnp.zeros((tm, 2 * LANES), F32)
        i1 = jnp.zeros((tm, 2 * LANES), F32)
        i2 = jnp.zeros((tm, 2 * LANES), F32)
        for k in range(k_top):
            ma = jnp.max(sa, axis=-1, keepdims=True)
            ia = jnp.min(jnp.where(sa == ma, lane_f, float(LANES)), axis=-1, keepdims=True)
            sa = jnp.where(lane_f == ia, neg_inf, sa)
            mb = jnp.max(sb, axis=-1, keepdims=True)
            ib = jnp.min(jnp.where(sb == mb, lane_f, float(LANES)), axis=-1, keepdims=True)
            sb = jnp.where(lane_f == ib, neg_inf, sb)
            s1 = jnp.where(lane2_hi == k, ma, s1)
            i1 = jnp.where(lane2_hi == k, ia, i1)
            s2 = jnp.where(lane2_lo == k, mb, s2)
            i2 = jnp.where(lane2_lo == k, ib, i2)
        cand = s1 + s2
        code = pos_code + (i1 * float(PEER_NKEYS) + i2)
        base = hd * k_top
        scv = jnp.full((tm, LANES), neg_inf, F32)
        ecol = jnp.zeros((tm, LANES), jnp.int32)
        for k in range(k_top):
            m = jnp.max(cand, axis=-1, keepdims=True)
            cm = jnp.min(jnp.where(cand == m, code, big), axis=-1, keepdims=True)
            cand = jnp.where(code == cm, neg_inf, cand)
            here = lane_i == base + k
            scv = jnp.where(here, m, scv)
            ecol = jnp.where(here, cm.astype(jnp.int32) & (PEER_EXPERTS - 1), ecol)
        mx = jnp.max(scv, axis=-1, keepdims=True)
        ex = jnp.exp(scv - mx)
        gate_ref[...] += ex / jnp.sum(ex, axis=-1, keepdims=True)
        e_ref[...] += ecol
        return 0

    lax.fori_loop(0, PEER_HEADS, head, 0)


def _peer_route(h1, g, wq, km):
    n = h1.shape[0]
    tm = TOK_TILE
    row = lambda i: (i, 0)
    return pl.pallas_call(
        _peer_route_kernel,
        grid=(n // tm,),
        in_specs=[pl.BlockSpec((tm, D_MODEL), row), pl.BlockSpec((1, D_MODEL), lambda i: (0, 0)),
                  pl.BlockSpec((PEER_HEADS, D_MODEL, LANES), lambda i: (0, 0, 0)),
                  pl.BlockSpec((PEER_HEADS, LANES, 2 * LANES), lambda i: (0, 0, 0))],
        out_specs=[pl.BlockSpec((tm, D_MODEL), row), pl.BlockSpec((tm, LANES), row),
                   pl.BlockSpec((tm, LANES), row)],
        out_shape=[jax.ShapeDtypeStruct((n, D_MODEL), BF16),
                   jax.ShapeDtypeStruct((n, LANES), jnp.int32),
                   jax.ShapeDtypeStruct((n, LANES), F32)],
        compiler_params=pltpu.CompilerParams(dimension_semantics=("parallel",),
                                             vmem_limit_bytes=VMEM_LIMIT),
        name="peer_route",
    )(h1, g, wq, km)


def _peer_key_matrix(sub_keys):
    k0 = jnp.transpose(sub_keys[:, 0], (0, 2, 1))
    k1 = jnp.transpose(sub_keys[:, 1], (0, 2, 1))
    z = jnp.zeros_like(k0)
    return jnp.concatenate([jnp.concatenate([k0, z], axis=2), jnp.concatenate([z, k1], axis=2)],
                           axis=1).astype(BF16)


def _peer_dense_kernel(xn_ref, e_ref, g_ref, u_ref, v_ref, h1_ref, o_ref, gs_ref, acc_ref):
    tt = xn_ref.shape[0]
    ce = u_ref.shape[0]
    stride = tt + GS_PAD
    c = pl.program_id(1)

    @pl.when(c == 0)
    def _():
        acc_ref[...] = jnp.zeros_like(acc_ref)
        sub = lax.broadcasted_iota(jnp.int32, (PEER_NKEYS, LANES), 0)

        def tok(t, _):
            e_row = e_ref[pl.ds(t, 1), :]
            g_row = g_ref[pl.ds(t, 1), :]
            i1 = e_row // PEER_NKEYS
            i2 = e_row % PEER_NKEYS
            left = jnp.where(sub == i1, g_row, 0.0).astype(BF16)
            right = jnp.where(sub == i2, 1.0, 0.0).astype(BF16)
            gs_ref[pl.ds(t, PEER_NKEYS, stride=stride), :] = _dot_nt(left, right)
            return 0

        lax.fori_loop(0, tt, tok, 0)

    hid = _dot_nt(xn_ref[...], u_ref[...])
    slabs = []
    for al in range(ce // LANES):
        start = pl.multiple_of((c * (ce // LANES) + al) * stride, 8)
        slabs.append(gs_ref[pl.ds(start, tt), :])
    gate = jnp.concatenate(slabs, axis=1)
    a = (gate * _gelu(hid)).astype(BF16)
    acc_ref[...] += _dot(a, v_ref[...])

    @pl.when(c == pl.num_programs(1) - 1)
    def _():
        o_ref[...] = h1_ref[...] + acc_ref[...]


def _peer_dense(xn, e, g, u, v, h1):
    n = xn.shape[0]
    tt = TOK_TILE
    ce = PEER_CHUNK
    row = lambda i, c: (i, 0)
    chunk = lambda i, c: (c, 0)
    return pl.pallas_call(
        _peer_dense_kernel,
        grid=(n // tt, PEER_EXPERTS // ce),
        in_specs=[pl.BlockSpec((tt, D_MODEL), row), pl.BlockSpec((tt, LANES), row),
                  pl.BlockSpec((tt, LANES), row), pl.BlockSpec((ce, D_MODEL), chunk),
                  pl.BlockSpec((ce, D_MODEL), chunk), pl.BlockSpec((tt, D_MODEL), row)],
        out_specs=pl.BlockSpec((tt, D_MODEL), row),
        out_shape=jax.ShapeDtypeStruct((n, D_MODEL), F32),
        scratch_shapes=[pltpu.VMEM((PEER_NKEYS * (tt + GS_PAD), LANES), F32),
                        pltpu.VMEM((tt, D_MODEL), F32)],
        compiler_params=pltpu.CompilerParams(dimension_semantics=("parallel", "arbitrary"),
                                             vmem_limit_bytes=VMEM_LIMIT),
        name="peer_dense",
    )(xn, e, g, u, v, h1)


def _ple_kernel(h_ref, p_ref, g_ref, wg_ref, bg_ref, wp_ref, gf_ref, o_ref, *, final):
    h = h_ref[...]
    xn = _rms(h, g_ref[...]).astype(BF16)
    gate = jax.nn.sigmoid(_dot(xn, wg_ref[...]) + bg_ref[...])
    out = h + _dot(p_ref[...].astype(BF16), wp_ref[...]) * gate
    if final:
        out = _rms(out, gf_ref[...])
    o_ref[...] = out


def _ple(h2, p, g, wg, bg, wp, gf, final):
    n = h2.shape[0]
    tm = TOK_TILE
    row = lambda i: (i, 0)
    fixed = lambda i: (0, 0)
    return pl.pallas_call(
        functools.partial(_ple_kernel, final=final),
        grid=(n // tm,),
        in_specs=[pl.BlockSpec((tm, D_MODEL), row), pl.BlockSpec((tm, PLE_DIM), row),
                  pl.BlockSpec((1, D_MODEL), fixed), pl.BlockSpec((D_MODEL, D_MODEL), fixed),
                  pl.BlockSpec((1, D_MODEL), fixed), pl.BlockSpec((PLE_DIM, D_MODEL), fixed),
                  pl.BlockSpec((1, D_MODEL), fixed)],
        out_specs=pl.BlockSpec((tm, D_MODEL), row),
        out_shape=jax.ShapeDtypeStruct((n, D_MODEL), F32),
        compiler_params=pltpu.CompilerParams(dimension_semantics=("parallel",),
                                             vmem_limit_bytes=VMEM_LIMIT),
        name="ple",
    )(h2, p, g, wg, bg, wp, gf)


def kernel(x_prompt, x_sample, cache_k, cache_v, p_prompt, p_sample, g_mix, w_in, g_sb_out, w_spatial, b_spatial, g_a_v, g_a_out, w_out, g_ffn, w_peer_q, peer_sub_keys, peer_u, peer_v, g_ple, w_ple_gate, b_ple_gate, w_ple, g_final):
    depth = w_in.shape[0]
    b, t, d = x_prompt.shape
    bs, ts, _ = x_sample.shape
    past = cache_k.shape[2]
    n_p, n_s = b * t, bs * ts
    assert d == D_MODEL and t % SB_BLOCK == 0 and past % SB_BLOCK == 0
    assert n_p % TOK_TILE == 0 and n_s % TOK_TILE == 0 and MIX_L % ts == 0 and n_p % ts == 0

    h = jnp.concatenate([x_prompt.reshape(n_p, d), x_sample.reshape(n_s, d)], axis=0)
    row = lambda a: a.reshape(1, -1)
    outs = {k: [] for k in ("kp", "vp", "ks", "vs", "vas")}
    for l in range(depth):
        q, kf, vf, kb, vb, ua, va = _inproj(h, row(g_mix[l]), w_in[l].astype(BF16))
        o_p = _sb_prompt(q, kb, vb, b, t)
        o_s = _sb_sample(q, kb, vb, cache_k[l].reshape(bs, past, SB_W), cache_v[l].reshape(bs, past, SB_W),
                         n_p, bs, ts)
        ws, bsp = _spatial_params(w_spatial[l], b_spatial[l], ts)
        h1, vn = _mixer_out(h, ua, va, jnp.concatenate([o_p, o_s], axis=0), ws, bsp,
                            row(g_a_v[l]), row(g_a_out[l]), row(g_sb_out[l]), w_out[l].astype(BF16), n_p)
        wq = jnp.transpose(w_peer_q[l].reshape(d, PEER_HEADS, LANES), (1, 0, 2)).astype(BF16)
        xn, e, gate = _peer_route(h1, row(g_ffn[l]), wq, _peer_key_matrix(peer_sub_keys[l]))
        h2 = _peer_dense(xn, e, gate, peer_u[l].astype(BF16), peer_v[l].astype(BF16), h1)
        p_l = jnp.concatenate([p_prompt[l].reshape(n_p, PLE_DIM), p_sample[l].reshape(n_s, PLE_DIM)], axis=0)
        h = _ple(h2, p_l, row(g_ple[l]), w_ple_gate[l].astype(BF16), row(b_ple_gate[l]),
                 w_ple[l].astype(BF16), row(g_final), final=(l == depth - 1))
        outs["kp"].append(kf[:n_p].reshape(b, t, N_HEADS, HEAD_DIM))
        outs["vp"].append(vf[:n_p].reshape(b, t, N_HEADS, HEAD_DIM))
        outs["ks"].append(kf[n_p:].reshape(bs, ts, N_HEADS, HEAD_DIM))
        outs["vs"].append(vf[n_p:].reshape(bs, ts, N_HEADS, HEAD_DIM))
        outs["vas"].append(vn[n_p:].reshape(bs, ts, A_W // GROUP, GROUP))
    return (h[:n_p].reshape(b, t, d), h[n_p:].reshape(bs, ts, d),
            jnp.stack(outs["kp"]), jnp.stack(outs["vp"]), jnp.stack(outs["ks"]), jnp.stack(outs["vs"]),
            jnp.stack(outs["vas"]))
```

```python
import functools

import jax
import jax.numpy as jnp
from jax import lax
from jax.experimental import pallas as pl
from jax.experimental.pallas import tpu as pltpu

F32 = jnp.float32
BF16 = jnp.bfloat16

LANES = 128
SUBLANES = 8
D_MODEL = 1024
N_HEADS = 8
HEAD_DIM = 64
SB_W = N_HEADS * HEAD_DIM
A_W = 512
IN_W = 3 * SB_W + 2 * A_W
GROUP = 64
MIX_L = 128
PEER_HEADS = 8
PEER_NKEYS = 128
PEER_TOPK = 16
PEER_EXPERTS = PEER_NKEYS * PEER_NKEYS
PLE_DIM = 256
RMS_EPS = 1e-6
TOK_TILE = 256
ROUTE_TILE = SUBLANES * LANES
SB_BLOCK = 256
SB_STEP_HEADS = 8
PEER_CHUNK = 1024
BUILD_UNROLL = 32
GS_PAD = 8
VMEM_LIMIT = 48 * 1024 * 1024
INV_SQRT2 = 0.7071067811865476


def _dot(a, b):
    return jnp.dot(a, b, preferred_element_type=F32)


def _dot_nt(a, b):
    return lax.dot_general(a, b, (((1,), (1,)), ((), ())), preferred_element_type=F32)


def _split_bf16(x):
    hi = x.astype(BF16)
    lo = (x - hi.astype(F32)).astype(BF16)
    return hi, lo


def _gelu(x):
    return 0.5 * x * (1.0 + lax.erf(x * INV_SQRT2))


def _rms(x, g):
    ms = jnp.mean(x * x, axis=-1, keepdims=True)
    return x * lax.rsqrt(ms + RMS_EPS) * g


def _group_mean_matrix():
    r = lax.broadcasted_iota(jnp.int32, (LANES, LANES), 0) // GROUP
    c = lax.broadcasted_iota(jnp.int32, (LANES, LANES), 1) // GROUP
    return jnp.where(r == c, 1.0 / GROUP, 0.0).astype(BF16)


def _group_rms(x, g, bd):
    xx = x * x
    hi, lo = _split_bf16(xx)
    parts = []
    for j in range(x.shape[1] // LANES):
        sl = slice(LANES * j, LANES * (j + 1))
        parts.append(_dot(hi[:, sl], bd) + _dot(lo[:, sl], bd))
    ms = jnp.concatenate(parts, axis=1)
    return x * lax.rsqrt(ms + RMS_EPS) * g


def _inproj_kernel(x_ref, g_ref, w_ref, q_ref, kf_ref, vf_ref, kb_ref, vb_ref, ua_ref, va_ref):
    xn = _rms(x_ref[...], g_ref[...]).astype(BF16)
    q = _dot(xn, w_ref[:, 0:SB_W])
    q_ref[...] = (q * (HEAD_DIM ** -0.5)).astype(BF16)
    k = _dot(xn, w_ref[:, SB_W:2 * SB_W])
    kf_ref[...] = k
    kb_ref[...] = k.astype(BF16)
    v = _dot(xn, w_ref[:, 2 * SB_W:3 * SB_W])
    vf_ref[...] = v
    vb_ref[...] = v.astype(BF16)
    ua_ref[...] = _dot(xn, w_ref[:, 3 * SB_W:3 * SB_W + A_W])
    va_ref[...] = _dot(xn, w_ref[:, 3 * SB_W + A_W:IN_W])


def _inproj(h, g, w):
    n = h.shape[0]
    tm = TOK_TILE
    row = lambda i: (i, 0)
    fixed = lambda i: (0, 0)
    out_f = jax.ShapeDtypeStruct((n, SB_W), F32)
    out_b = jax.ShapeDtypeStruct((n, SB_W), BF16)
    blk = pl.BlockSpec((tm, SB_W), row)
    return pl.pallas_call(
        _inproj_kernel,
        grid=(n // tm,),
        in_specs=[pl.BlockSpec((tm, D_MODEL), row), pl.BlockSpec((1, D_MODEL), fixed),
                  pl.BlockSpec((D_MODEL, IN_W), fixed)],
        out_specs=[blk] * 7,
        out_shape=[out_b, out_f, out_f, out_b, out_b, out_f, out_f],
        compiler_params=pltpu.CompilerParams(dimension_semantics=("parallel",),
                                             vmem_limit_bytes=VMEM_LIMIT),
        name="inproj",
    )(h, g, w)


def _strict_lower_ones(n):
    r = lax.broadcasted_iota(jnp.int32, (n, n), 0)
    c = lax.broadcasted_iota(jnp.int32, (n, n), 1)
    return jnp.where(r > c, 1.0, 0.0).astype(BF16)


def _sb_blocks(chains, ustrict, mask):
    zs = [_dot_nt(qh, k2) for qh, k2, _, _ in chains]
    log_betas, log_1ms, splits = [], [], []
    for z in zs:
        sp = jnp.log(1.0 + jnp.exp(-jnp.abs(z)))
        log_beta = jnp.minimum(z, 0.0) - sp
        log_1m = log_beta - z
        if mask is not None:
            log_1m = jnp.where(mask, log_1m, 0.0)
        log_betas.append(log_beta)
        log_1ms.append(log_1m)
        splits.append(_split_bf16(log_1m))
    laters = [_dot(hi, ustrict) + _dot(lo, ustrict) for hi, lo in splits]
    ws, carries = [], []
    for (_, k2, _, carry), log_beta, log_1m, later in zip(chains, log_betas, log_1ms, laters):
        rest = later + jnp.concatenate([carry] * (k2.shape[0] // LANES), axis=1)
        w = jnp.exp(log_beta + rest)
        if mask is not None:
            w = jnp.where(mask, w, 0.0)
        ws.append(w.astype(BF16))
        carries.append(carry + jnp.broadcast_to(later[:, :1] + log_1m[:, :1], carry.shape))
    return [(_dot(w, v2h), carry) for w, (_, _, v2h, _), carry in zip(ws, chains, carries)]


def _sb_prompt_kernel(q_ref, k_ref, v_ref, o_ref, acc_ref, carry_ref):
    tq = q_ref.shape[0]
    tk = tq
    qi = pl.program_id(2)
    ustrict = _strict_lower_ones(tk)
    r = lax.broadcasted_iota(jnp.int32, (tq, tk), 0)
    c = lax.broadcasted_iota(jnp.int32, (tq, tk), 1)
    diag = c < r
    lane_head = lax.broadcasted_iota(jnp.int32, (1, LANES), 1) // HEAD_DIM
    heads = [lane_head == hh for hh in range(2)]
    n_pairs = q_ref.shape[1] // LANES
    lanes = [slice(pr * LANES, (pr + 1) * LANES) for pr in range(n_pairs)]
    qhs = [[jnp.where(hm, q_ref[:, ln], jnp.zeros((tq, LANES), BF16)) for hm in heads] for ln in lanes]
    acc_ref[...] = jnp.zeros_like(acc_ref)
    carry_ref[...] = jnp.zeros_like(carry_ref)

    def step(kb, mask):
        start = pl.multiple_of(kb * tk, tk)
        chains = []
        for pr, ln in enumerate(lanes):
            k2 = k_ref[pl.ds(start, tk), ln]
            v2 = v_ref[pl.ds(start, tk), ln]
            for hh in range(2):
                v2h = jnp.where(heads[hh], v2, jnp.zeros_like(v2))
                chains.append((qhs[pr][hh], k2, v2h, carry_ref[2 * pr + hh]))
        results = _sb_blocks(chains, ustrict, mask)
        for i, (_, carry) in enumerate(results):
            carry_ref[i] = carry
        for pr, ln in enumerate(lanes):
            acc_ref[:, ln] += results[2 * pr][0] + results[2 * pr + 1][0]

    step(qi, diag)

    def body(it, _):
        step(qi - 1 - it, None)
        return 0

    lax.fori_loop(0, qi, body, 0)
    o_ref[...] = acc_ref[...]


def _sb_prompt(q, kb, vb, batch, seq):
    tq = SB_BLOCK
    nq = seq // tq
    width = SB_STEP_HEADS * HEAD_DIM
    return pl.pallas_call(
        _sb_prompt_kernel,
        grid=(batch, SB_W // width, nq),
        in_specs=[pl.BlockSpec((tq, width), lambda b, j, i: (b * nq + i, j)),
                  pl.BlockSpec((seq, width), lambda b, j, i: (b, j)),
                  pl.BlockSpec((seq, width), lambda b, j, i: (b, j))],
        out_specs=pl.BlockSpec((tq, width), lambda b, j, i: (b * nq + i, j)),
        out_shape=jax.ShapeDtypeStruct((batch * seq, SB_W), F32),
        scratch_shapes=[pltpu.VMEM((tq, width), F32), pltpu.VMEM((SB_STEP_HEADS, tq, LANES), F32)],
        compiler_params=pltpu.CompilerParams(
            dimension_semantics=("parallel", "parallel", "arbitrary"), vmem_limit_bytes=VMEM_LIMIT),
        name="sb_prompt",
    )(q, kb, vb)


def _sb_sample_kernel(q_ref, k_ref, v_ref, ck_ref, cv_ref, o_ref):
    ts = q_ref.shape[0]
    past = ck_ref.shape[0]
    tk = SB_BLOCK
    ustrict = _strict_lower_ones(tk)
    ustrict_d = ustrict[:LANES, :LANES]
    r = lax.broadcasted_iota(jnp.int32, (ts, LANES), 0)
    c = lax.broadcasted_iota(jnp.int32, (ts, LANES), 1)
    diag = c < r
    lane_head = lax.broadcasted_iota(jnp.int32, (1, LANES), 1) // HEAD_DIM
    q2 = q_ref[...]
    pad = jnp.zeros((LANES - ts, LANES), BF16)
    kn = jnp.concatenate([k_ref[...], pad], axis=0)
    vn = jnp.concatenate([v_ref[...], pad], axis=0)
    heads = [lane_head == hh for hh in range(2)]
    qhs = [jnp.where(hm, q2, jnp.zeros_like(q2)) for hm in heads]
    zero = jnp.zeros((ts, LANES), F32)
    results = _sb_blocks([(qh, kn, jnp.where(hm, vn, jnp.zeros_like(vn)), zero) for qh, hm in zip(qhs, heads)],
                         ustrict_d, diag)
    acc = results[0][0] + results[1][0]
    for kb in range(past // tk - 1, -1, -1):
        k2 = ck_ref[kb * tk:(kb + 1) * tk, :].astype(BF16)
        v2 = cv_ref[kb * tk:(kb + 1) * tk, :].astype(BF16)
        results = _sb_blocks([(qh, k2, jnp.where(hm, v2, jnp.zeros_like(v2)), carry)
                              for qh, hm, (_, carry) in zip(qhs, heads, results)], ustrict, None)
        acc = acc + results[0][0] + results[1][0]
    o_ref[...] = acc


def _sb_sample(q, kb, vb, cache_k, cache_v, n_prompt, batch, ts):
    past = cache_k.shape[1]
    pairs = SB_W // LANES
    off = n_prompt // ts
    new = pl.BlockSpec((ts, LANES), lambda b, j: (off + b, j))
    old = pl.BlockSpec((None, past, LANES), lambda b, j: (b, 0, j))
    return pl.pallas_call(
        _sb_sample_kernel,
        grid=(batch, pairs),
        in_specs=[new, new, new, old, old],
        out_specs=pl.BlockSpec((ts, LANES), lambda b, j: (b, j)),
        out_shape=jax.ShapeDtypeStruct((batch * ts, SB_W), F32),
        compiler_params=pltpu.CompilerParams(dimension_semantics=("parallel", "parallel"),
                                             vmem_limit_bytes=VMEM_LIMIT),
        name="sb_sample",
    )(q, kb, vb, cache_k, cache_v)


def _mixer_out_kernel(h_ref, ua_ref, va_ref, o_ref, ws_ref, bs_ref, gav_ref, gao_ref, gsb_ref, wo_ref,
                      h1_ref, vn_ref):
    tm = h_ref.shape[0]
    bd = _group_mean_matrix()
    lane_group = lax.broadcasted_iota(jnp.int32, (1, LANES), 1) // GROUP
    u = _gelu(ua_ref[...])
    vn = _group_rms(_gelu(va_ref[...]), gav_ref[...], bd)
    vn_ref[...] = vn
    vnb = vn.astype(BF16)
    rows = []
    for rr in range(tm // MIX_L):
        slabs = []
        for j in range(A_W // LANES):
            v2 = vnb[rr * MIX_L:(rr + 1) * MIX_L, j * LANES:(j + 1) * LANES]
            mix = jnp.zeros((MIX_L, LANES), F32)
            for gg in range(2):
                v2g = jnp.where(lane_group == gg, v2, jnp.zeros_like(v2))
                mix = mix + _dot(ws_ref[2 * j + gg], v2g)
            slabs.append(mix)
        rows.append(jnp.concatenate(slabs, axis=1) + bs_ref[...])
    mix = jnp.concatenate(rows, axis=0)
    oa = _group_rms(u * mix, gao_ref[...], bd)
    osb = _group_rms(o_ref[...], gsb_ref[...], bd)
    cat = jnp.concatenate([osb, oa], axis=1).astype(BF16)
    h1_ref[...] = h_ref[...] + _dot(cat, wo_ref[...])


def _mixer_out(h, ua, va, o, ws, bs, gav, gao, gsb, wo, n_prompt):
    n = h.shape[0]
    tm = TOK_TILE
    npt = n_prompt // tm
    row = lambda i: (i, 0)
    fixed = lambda i: (0, 0)
    variant = lambda i: jnp.where(i < npt, 0, 1)
    return pl.pallas_call(
        _mixer_out_kernel,
        grid=(n // tm,),
        in_specs=[pl.BlockSpec((tm, D_MODEL), row), pl.BlockSpec((tm, A_W), row),
                  pl.BlockSpec((tm, A_W), row), pl.BlockSpec((tm, SB_W), row),
                  pl.BlockSpec((None, 2 * A_W // LANES, MIX_L, MIX_L), lambda i: (variant(i), 0, 0, 0)),
                  pl.BlockSpec((None, MIX_L, A_W), lambda i: (variant(i), 0, 0)),
                  pl.BlockSpec((1, A_W), fixed), pl.BlockSpec((1, A_W), fixed),
                  pl.BlockSpec((1, SB_W), fixed), pl.BlockSpec((D_MODEL, D_MODEL), fixed)],
        out_specs=[pl.BlockSpec((tm, D_MODEL), row), pl.BlockSpec((tm, A_W), row)],
        out_shape=[jax.ShapeDtypeStruct((n, D_MODEL), F32), jax.ShapeDtypeStruct((n, A_W), F32)],
        compiler_params=pltpu.CompilerParams(dimension_semantics=("parallel",),
                                             vmem_limit_bytes=VMEM_LIMIT),
        name="mixer_out",
    )(h, ua, va, o, ws, bs, gav, gao, gsb, wo)


def _spatial_params(w_s, b_s, ts):
    tril = jnp.tril(jnp.ones((MIX_L, MIX_L), bool))
    w_p = jnp.where(tril[None], w_s, 0.0)
    rep = MIX_L // ts
    w_small = jnp.where(tril[None, :ts, :ts], w_s[:, :ts, :ts], 0.0)
    eye = jnp.eye(rep, dtype=w_s.dtype)
    w_smp = jnp.einsum("ab,gts->gatbs", eye, w_small).reshape(-1, MIX_L, MIX_L)
    b_p = jnp.repeat(jnp.transpose(b_s), A_W // b_s.shape[0], axis=1)
    b_smp = jnp.tile(b_p[:ts], (rep, 1))
    return jnp.stack([w_p, w_smp]).astype(BF16), jnp.stack([b_p, b_smp])


def _argmax_tree(leaf, lo, hi):
    if hi - lo == 1:
        return leaf(lo)
    mid = (lo + hi) // 2
    va, ta = _argmax_tree(leaf, lo, mid)
    vb, tb = _argmax_tree(leaf, mid, hi)
    return jnp.maximum(va, vb), jnp.where(va >= vb, ta, tb)


def _vreg_rows(idx):
    if isinstance(idx, int):
        return slice(idx * SUBLANES, (idx + 1) * SUBLANES)
    return pl.ds(pl.multiple_of(idx * SUBLANES, SUBLANES), SUBLANES)


_PEER_PAIRS = [(i, j) for i in range(PEER_TOPK) for j in range(PEER_TOPK) if (i + 1) * (j + 1) <= PEER_TOPK]


def _peer_route_kernel(h_ref, g_ref, wq_ref, km_ref, xn_ref, e_ref, gate_ref,
                       s_ref, topv_ref, topi_ref, cv_ref, ce_ref, sc_ref, rese_ref, resg_ref):
    k_top = PEER_TOPK
    xn = _rms(h_ref[...], g_ref[...]).astype(BF16)
    xn_ref[...] = xn
    neg_inf = -jnp.inf
    none_yet = jnp.full((SUBLANES, LANES), -1, jnp.int32)

    def head(hd, _):
        pq = _dot(xn, wq_ref[hd]).astype(BF16)
        for c in range(SUBLANES):
            st = _dot_nt(km_ref[hd], pq[c * LANES:(c + 1) * LANES])
            s_ref[pl.ds(c, 2 * PEER_NKEYS, stride=SUBLANES), :] = st

        for p in range(2):
            def extract(k, prev, p=p):
                def leaf(n):
                    r = _vreg_rows(p * PEER_NKEYS + n)
                    v = jnp.where(prev == n, neg_inf, s_ref[r, :])
                    s_ref[r, :] = v
                    return v, n
                m, im = _argmax_tree(leaf, 0, PEER_NKEYS)
                topv_ref[_vreg_rows(p * k_top + k), :] = m
                topi_ref[_vreg_rows(p * k_top + k), :] = im
                return im
            lax.fori_loop(0, k_top, extract, none_yet)

        for idx, (i, j) in enumerate(_PEER_PAIRS):
            cv_ref[_vreg_rows(idx), :] = topv_ref[_vreg_rows(i), :] + topv_ref[_vreg_rows(k_top + j), :]
            ce_ref[_vreg_rows(idx), :] = (topi_ref[_vreg_rows(i), :] * PEER_NKEYS
                                          + topi_ref[_vreg_rows(k_top + j), :])

        def pick(k, prev):
            def leaf(idx):
                r = _vreg_rows(idx)
                eid = ce_ref[r, :]
                v = jnp.where(eid == prev, neg_inf, cv_ref[r, :])
                cv_ref[r, :] = v
                return v, eid
            m, em = _argmax_tree(leaf, 0, len(_PEER_PAIRS))
            sc_ref[_vreg_rows(k), :] = m
            rese_ref[_vreg_rows(hd * k_top + k), :] = em.astype(F32)
            return em
        lax.fori_loop(0, k_top, pick, none_yet)

        mx = sc_ref[_vreg_rows(0), :]
        exs = [jnp.exp(sc_ref[_vreg_rows(k), :] - mx) for k in range(k_top)]
        total = exs[0]
        for ex in exs[1:]:
            total = total + ex
        for k in range(k_top):
            resg_ref[_vreg_rows(hd * k_top + k), :] = exs[k] / total
        return 0

    lax.fori_loop(0, PEER_HEADS, head, 0)
    for c in range(SUBLANES):
        rows = slice(c * LANES, (c + 1) * LANES)
        e_ref[rows, :] = jnp.transpose(rese_ref[pl.ds(c, LANES, stride=SUBLANES), :]).astype(jnp.int32)
        gate_ref[rows, :] = jnp.transpose(resg_ref[pl.ds(c, LANES, stride=SUBLANES), :])


def _peer_route(h1, g, wq, km):
    n = h1.shape[0]
    tm = ROUTE_TILE
    row = lambda i: (i, 0)
    vregs = lambda count, dtype: pltpu.VMEM((count * SUBLANES, LANES), dtype)
    return pl.pallas_call(
        _peer_route_kernel,
        grid=(n // tm,),
        in_specs=[pl.BlockSpec((tm, D_MODEL), row), pl.BlockSpec((1, D_MODEL), lambda i: (0, 0)),
                  pl.BlockSpec((PEER_HEADS, D_MODEL, LANES), lambda i: (0, 0, 0)),
                  pl.BlockSpec((PEER_HEADS, 2 * PEER_NKEYS, LANES), lambda i: (0, 0, 0))],
        out_specs=[pl.BlockSpec((tm, D_MODEL), row), pl.BlockSpec((tm, LANES), row),
                   pl.BlockSpec((tm, LANES), row)],
        out_shape=[jax.ShapeDtypeStruct((n, D_MODEL), BF16),
                   jax.ShapeDtypeStruct((n, LANES), jnp.int32),
                   jax.ShapeDtypeStruct((n, LANES), F32)],
        scratch_shapes=[vregs(2 * PEER_NKEYS, F32), vregs(2 * PEER_TOPK, F32), vregs(2 * PEER_TOPK, jnp.int32),
                        vregs(len(_PEER_PAIRS), F32), vregs(len(_PEER_PAIRS), jnp.int32),
                        vregs(PEER_TOPK, F32), vregs(PEER_HEADS * PEER_TOPK, F32),
                        vregs(PEER_HEADS * PEER_TOPK, F32)],
        compiler_params=pltpu.CompilerParams(dimension_semantics=("parallel",),
                                             vmem_limit_bytes=VMEM_LIMIT),
        name="peer_route",
    )(h1, g, wq, km)


def _peer_key_matrix(sub_keys):
    z = jnp.zeros_like(sub_keys[:, 0])
    k0 = jnp.concatenate([sub_keys[:, 0], z], axis=2)
    k1 = jnp.concatenate([z, sub_keys[:, 1]], axis=2)
    return jnp.concatenate([k0, k1], axis=1).astype(BF16)


def _peer_dense_kernel(xn_ref, e_ref, g_ref, u_ref, v_ref, h1_ref, o_ref, gs_ref, acc_ref):
    tt = xn_ref.shape[0]
    ce = u_ref.shape[0]
    stride = tt + GS_PAD
    c = pl.program_id(1)

    @pl.when(c == 0)
    def _():
        acc_ref[...] = jnp.zeros_like(acc_ref)
        sub = lax.broadcasted_iota(jnp.int32, (PEER_NKEYS, LANES), 0)

        def tok(t, _):
            e_row = e_ref[pl.ds(t, 1), :]
            g_row = g_ref[pl.ds(t, 1), :]
            i1 = e_row // PEER_NKEYS
            i2 = e_row % PEER_NKEYS
            left = jnp.where(sub == i1, g_row, 0.0).astype(BF16)
            right = jnp.where(sub == i2, 1.0, 0.0).astype(BF16)
            gs_ref[pl.ds(t, PEER_NKEYS, stride=stride), :] = _dot_nt(left, right)
            return 0

        lax.fori_loop(0, tt, tok, 0, unroll=BUILD_UNROLL)

    hid = _dot_nt(xn_ref[...], u_ref[...])
    slabs = []
    for al in range(ce // LANES):
        start = pl.multiple_of((c * (ce // LANES) + al) * stride, 8)
        slabs.append(gs_ref[pl.ds(start, tt), :])
    gate = jnp.concatenate(slabs, axis=1)
    a = (gate * _gelu(hid)).astype(BF16)
    acc_ref[...] += _dot(a, v_ref[...])

    @pl.when(c == pl.num_programs(1) - 1)
    def _():
        o_ref[...] = h1_ref[...] + acc_ref[...]


def _peer_dense(xn, e, g, u, v, h1):
    n = xn.shape[0]
    tt = TOK_TILE
    ce = PEER_CHUNK
    row = lambda i, c: (i, 0)
    chunk = lambda i, c: (c, 0)
    return pl.pallas_call(
        _peer_dense_kernel,
        grid=(n // tt, PEER_EXPERTS // ce),
        in_specs=[pl.BlockSpec((tt, D_MODEL), row), pl.BlockSpec((tt, LANES), row),
                  pl.BlockSpec((tt, LANES), row), pl.BlockSpec((ce, D_MODEL), chunk),
                  pl.BlockSpec((ce, D_MODEL), chunk), pl.BlockSpec((tt, D_MODEL), row)],
        out_specs=pl.BlockSpec((tt, D_MODEL), row),
        out_shape=jax.ShapeDtypeStruct((n, D_MODEL), F32),
        scratch_shapes=[pltpu.VMEM((PEER_NKEYS * (tt + GS_PAD), LANES), F32),
                        pltpu.VMEM((tt, D_MODEL), F32)],
        compiler_params=pltpu.CompilerParams(dimension_semantics=("parallel", "arbitrary"),
                                             vmem_limit_bytes=VMEM_LIMIT),
        name="peer_dense",
    )(xn, e, g, u, v, h1)


def _ple_kernel(h_ref, p_ref, g_ref, wg_ref, bg_ref, wp_ref, gf_ref, o_ref, *, final):
    h = h_ref[...]
    xn = _rms(h, g_ref[...]).astype(BF16)
    gate = jax.nn.sigmoid(_dot(xn, wg_ref[...]) + bg_ref[...])
    out = h + _dot(p_ref[...].astype(BF16), wp_ref[...]) * gate
    if final:
        out = _rms(out, gf_ref[...])
    o_ref[...] = out


def _ple(h2, p, g, wg, bg, wp, gf, final):
    n = h2.shape[0]
    tm = TOK_TILE
    row = lambda i: (i, 0)
    fixed = lambda i: (0, 0)
    return pl.pallas_call(
        functools.partial(_ple_kernel, final=final),
        grid=(n // tm,),
        in_specs=[pl.BlockSpec((tm, D_MODEL), row), pl.BlockSpec((tm, PLE_DIM), row),
                  pl.BlockSpec((1, D_MODEL), fixed), pl.BlockSpec((D_MODEL, D_MODEL), fixed),
                  pl.BlockSpec((1, D_MODEL), fixed), pl.BlockSpec((PLE_DIM, D_MODEL), fixed),
                  pl.BlockSpec((1, D_MODEL), fixed)],
        out_specs=pl.BlockSpec((tm, D_MODEL), row),
        out_shape=jax.ShapeDtypeStruct((n, D_MODEL), F32),
        compiler_params=pltpu.CompilerParams(dimension_semantics=("parallel",),
                                             vmem_limit_bytes=VMEM_LIMIT),
        name="ple",
    )(h2, p, g, wg, bg, wp, gf)


def kernel(x_prompt, x_sample, cache_k, cache_v, p_prompt, p_sample, g_mix, w_in, g_sb_out, w_spatial, b_spatial, g_a_v, g_a_out, w_out, g_ffn, w_peer_q, peer_sub_keys, peer_u, peer_v, g_ple, w_ple_gate, b_ple_gate, w_ple, g_final):
    depth = w_in.shape[0]
    b, t, d = x_prompt.shape
    bs, ts, _ = x_sample.shape
    past = cache_k.shape[2]
    n_p, n_s = b * t, bs * ts
    assert d == D_MODEL and t % SB_BLOCK == 0 and past % SB_BLOCK == 0
    assert n_p % TOK_TILE == 0 and n_s % TOK_TILE == 0 and MIX_L % ts == 0 and n_p % ts == 0
    assert (n_p + n_s) % ROUTE_TILE == 0

    h = jnp.concatenate([x_prompt.reshape(n_p, d), x_sample.reshape(n_s, d)], axis=0)
    row = lambda a: a.reshape(1, -1)
    outs = {k: [] for k in ("kp", "vp", "ks", "vs", "vas")}
    for l in range(depth):
        q, kf, vf, kb, vb, ua, va = _inproj(h, row(g_mix[l]), w_in[l].astype(BF16))
        o_p = _sb_prompt(q, kb, vb, b, t)
        o_s = _sb_sample(q, kb, vb, cache_k[l].reshape(bs, past, SB_W), cache_v[l].reshape(bs, past, SB_W),
                         n_p, bs, ts)
        ws, bsp = _spatial_params(w_spatial[l], b_spatial[l], ts)
        h1, vn = _mixer_out(h, ua, va, jnp.concatenate([o_p, o_s], axis=0), ws, bsp,
                            row(g_a_v[l]), row(g_a_out[l]), row(g_sb_out[l]), w_out[l].astype(BF16), n_p)
        wq = jnp.transpose(w_peer_q[l].reshape(d, PEER_HEADS, LANES), (1, 0, 2)).astype(BF16)
        xn, e, gate = _peer_route(h1, row(g_ffn[l]), wq, _peer_key_matrix(peer_sub_keys[l]))
        h2 = _peer_dense(xn, e, gate, peer_u[l].astype(BF16), peer_v[l].astype(BF16), h1)
        p_l = jnp.concatenate([p_prompt[l].reshape(n_p, PLE_DIM), p_sample[l].reshape(n_s, PLE_DIM)], axis=0)
        h = _ple(h2, p_l, row(g_ple[l]), w_ple_gate[l].astype(BF16), row(b_ple_gate[l]),
                 w_ple[l].astype(BF16), row(g_final), final=(l == depth - 1))
        outs["kp"].append(kf[:n_p].reshape(b, t, N_HEADS, HEAD_DIM))
        outs["vp"].append(vf[:n_p].reshape(b, t, N_HEADS, HEAD_DIM))
        outs["ks"].append(kf[n_p:].reshape(bs, ts, N_HEADS, HEAD_DIM))
        outs["vs"].append(vf[n_p:].reshape(bs, ts, N_HEADS, HEAD_DIM))
        outs["vas"].append(vn[n_p:].reshape(bs, ts, A_W // GROUP, GROUP))
    return (h[:n_p].reshape(b, t, d), h[n_p:].reshape(bs, ts, d),
            jnp.stack(outs["kp"]), jnp.stack(outs["vp"]), jnp.stack(outs["ks"]), jnp.stack(outs["vs"]),
            jnp.stack(outs["vas"]))
```

```python
import functools

import jax
import jax.numpy as jnp
from jax import lax
from jax.experimental import pallas as pl
from jax.experimental.pallas import tpu as pltpu

F32 = jnp.float32
BF16 = jnp.bfloat16

LANES = 128
SUBLANES = 8
D_MODEL = 1024
N_HEADS = 8
HEAD_DIM = 64
SB_W = N_HEADS * HEAD_DIM
A_W = 512
IN_W = 3 * SB_W + 2 * A_W
GROUP = 64
MIX_L = 128
PEER_HEADS = 8
PEER_NKEYS = 128
PEER_TOPK = 16
PEER_EXPERTS = PEER_NKEYS * PEER_NKEYS
PLE_DIM = 256
RMS_EPS = 1e-6
TOK_TILE = 256
ROUTE_TILE = SUBLANES * LANES
SB_BLOCK = 256
SB_STEP_HEADS = 8
PEER_TILE = 512
PEER_CHUNK = 1024
PEER_VMEM_LIMIT = 58 * 1024 * 1024
BUILD_UNROLL = 32
GS_PAD = 8
VMEM_LIMIT = 48 * 1024 * 1024
INV_SQRT2 = 0.7071067811865476


def _dot(a, b):
    return jnp.dot(a, b, preferred_element_type=F32)


def _dot_nt(a, b):
    return lax.dot_general(a, b, (((1,), (1,)), ((), ())), preferred_element_type=F32)


def _split_bf16(x):
    hi = x.astype(BF16)
    lo = (x - hi.astype(F32)).astype(BF16)
    return hi, lo


def _gelu(x):
    return 0.5 * x * (1.0 + lax.erf(x * INV_SQRT2))


def _rms(x, g):
    ms = jnp.mean(x * x, axis=-1, keepdims=True)
    return x * lax.rsqrt(ms + RMS_EPS) * g


def _group_mean_matrix():
    r = lax.broadcasted_iota(jnp.int32, (LANES, LANES), 0) // GROUP
    c = lax.broadcasted_iota(jnp.int32, (LANES, LANES), 1) // GROUP
    return jnp.where(r == c, 1.0 / GROUP, 0.0).astype(BF16)


def _group_rms(x, g, bd):
    xx = x * x
    hi, lo = _split_bf16(xx)
    parts = []
    for j in range(x.shape[1] // LANES):
        sl = slice(LANES * j, LANES * (j + 1))
        parts.append(_dot(hi[:, sl], bd) + _dot(lo[:, sl], bd))
    ms = jnp.concatenate(parts, axis=1)
    return x * lax.rsqrt(ms + RMS_EPS) * g


def _inproj_kernel(x_ref, g_ref, w_ref, *refs, n_prompt_tiles):
    q_ref, kb_ref, vb_ref, ua_ref, va_ref, kp_ref, vp_ref, ks_ref, vs_ref = refs[-9:]
    xn = _rms(x_ref[...], g_ref[...]).astype(BF16)
    q = _dot(xn, w_ref[:, 0:SB_W])
    q_ref[...] = (q * (HEAD_DIM ** -0.5)).astype(BF16)
    k = _dot(xn, w_ref[:, SB_W:2 * SB_W])
    kb_ref[...] = k.astype(BF16)
    v = _dot(xn, w_ref[:, 2 * SB_W:3 * SB_W])
    vb_ref[...] = v.astype(BF16)
    ua_ref[...] = _dot(xn, w_ref[:, 3 * SB_W:3 * SB_W + A_W])
    va_ref[...] = _dot(xn, w_ref[:, 3 * SB_W + A_W:IN_W])
    is_prompt = pl.program_id(0) < n_prompt_tiles

    @pl.when(is_prompt)
    def _():
        kp_ref[...] = k
        vp_ref[...] = v

    @pl.when(jnp.logical_not(is_prompt))
    def _():
        ks_ref[...] = k
        vs_ref[...] = v


def _split_rows_specs(layer, tile, width, n_prompt_tiles):
    prompt = pl.BlockSpec((None, tile, width), lambda i: (layer, jnp.minimum(i, n_prompt_tiles - 1), 0))
    sample = pl.BlockSpec((None, tile, width), lambda i: (layer, jnp.maximum(i - n_prompt_tiles, 0), 0))
    return prompt, sample


def _inproj(h, g, w, layer, depth, n_prompt, stacked):
    n = h.shape[0]
    tm = TOK_TILE
    row = lambda i: (i, 0)
    fixed = lambda i: (0, 0)
    out_f = jax.ShapeDtypeStruct((n, SB_W), F32)
    out_b = jax.ShapeDtypeStruct((n, SB_W), BF16)
    stack_p = jax.ShapeDtypeStruct((depth, n_prompt, SB_W), F32)
    stack_s = jax.ShapeDtypeStruct((depth, n - n_prompt, SB_W), F32)
    blk = pl.BlockSpec((tm, SB_W), row)
    prompt_blk, sample_blk = _split_rows_specs(layer, tm, SB_W, n_prompt // tm)
    carried = [] if stacked is None else list(stacked)
    outs = pl.pallas_call(
        functools.partial(_inproj_kernel, n_prompt_tiles=n_prompt // tm),
        grid=(n // tm,),
        in_specs=[pl.BlockSpec((tm, D_MODEL), row), pl.BlockSpec((1, D_MODEL), fixed),
                  pl.BlockSpec((D_MODEL, IN_W), fixed)] + [pl.BlockSpec(memory_space=pl.ANY)] * len(carried),
        out_specs=[blk] * 5 + [prompt_blk, prompt_blk, sample_blk, sample_blk],
        out_shape=[out_b, out_b, out_b, out_f, out_f, stack_p, stack_p, stack_s, stack_s],
        input_output_aliases={3 + j: 5 + j for j in range(len(carried))},
        compiler_params=pltpu.CompilerParams(dimension_semantics=("arbitrary",),
                                             vmem_limit_bytes=VMEM_LIMIT),
        name="inproj",
    )(h, g, w, *carried)
    return outs[:5], tuple(outs[5:])


def _strict_lower_ones(n):
    r = lax.broadcasted_iota(jnp.int32, (n, n), 0)
    c = lax.broadcasted_iota(jnp.int32, (n, n), 1)
    return jnp.where(r > c, 1.0, 0.0).astype(BF16)


def _sb_blocks(chains, ustrict, mask):
    zs = [_dot_nt(qh, k2) for qh, k2, _, _ in chains]
    log_betas, log_1ms, splits = [], [], []
    for z in zs:
        sp = jnp.log(1.0 + jnp.exp(-jnp.abs(z)))
        log_beta = jnp.minimum(z, 0.0) - sp
        log_1m = log_beta - z
        if mask is not None:
            log_1m = jnp.where(mask, log_1m, 0.0)
        log_betas.append(log_beta)
        log_1ms.append(log_1m)
        splits.append(_split_bf16(log_1m))
    laters = [_dot(hi, ustrict) + _dot(lo, ustrict) for hi, lo in splits]
    ws, carries = [], []
    for (_, k2, _, carry), log_beta, log_1m, later in zip(chains, log_betas, log_1ms, laters):
        rest = later + jnp.concatenate([carry] * (k2.shape[0] // LANES), axis=1)
        w = jnp.exp(log_beta + rest)
        if mask is not None:
            w = jnp.where(mask, w, 0.0)
        ws.append(w.astype(BF16))
        carries.append(carry + jnp.broadcast_to(later[:, :1] + log_1m[:, :1], carry.shape))
    return [(_dot(w, v2h), carry) for w, (_, _, v2h, _), carry in zip(ws, chains, carries)]


def _sb_prompt_kernel(q_ref, k_ref, v_ref, o_ref, acc_ref, carry_ref):
    tq = q_ref.shape[0]
    tk = tq
    qi = pl.program_id(2)
    ustrict = _strict_lower_ones(tk)
    r = lax.broadcasted_iota(jnp.int32, (tq, tk), 0)
    c = lax.broadcasted_iota(jnp.int32, (tq, tk), 1)
    diag = c < r
    lane_head = lax.broadcasted_iota(jnp.int32, (1, LANES), 1) // HEAD_DIM
    heads = [lane_head == hh for hh in range(2)]
    n_pairs = q_ref.shape[1] // LANES
    lanes = [slice(pr * LANES, (pr + 1) * LANES) for pr in range(n_pairs)]
    qhs = [[jnp.where(hm, q_ref[:, ln], jnp.zeros((tq, LANES), BF16)) for hm in heads] for ln in lanes]
    acc_ref[...] = jnp.zeros_like(acc_ref)
    carry_ref[...] = jnp.zeros_like(carry_ref)

    def step(kb, mask):
        start = pl.multiple_of(kb * tk, tk)
        chains = []
        for pr, ln in enumerate(lanes):
            k2 = k_ref[pl.ds(start, tk), ln]
            v2 = v_ref[pl.ds(start, tk), ln]
            for hh in range(2):
                v2h = jnp.where(heads[hh], v2, jnp.zeros_like(v2))
                chains.append((qhs[pr][hh], k2, v2h, carry_ref[2 * pr + hh]))
        results = _sb_blocks(chains, ustrict, mask)
        for i, (_, carry) in enumerate(results):
            carry_ref[i] = carry
        for pr, ln in enumerate(lanes):
            acc_ref[:, ln] += results[2 * pr][0] + results[2 * pr + 1][0]

    step(qi, diag)

    def body(it, _):
        step(qi - 1 - it, None)
        return 0

    lax.fori_loop(0, qi, body, 0)
    o_ref[...] = acc_ref[...]


def _sb_prompt(q, kb, vb, batch, seq):
    tq = SB_BLOCK
    nq = seq // tq
    width = SB_STEP_HEADS * HEAD_DIM
    return pl.pallas_call(
        _sb_prompt_kernel,
        grid=(batch, SB_W // width, nq),
        in_specs=[pl.BlockSpec((tq, width), lambda b, j, i: (b * nq + i, j)),
                  pl.BlockSpec((seq, width), lambda b, j, i: (b, j)),
                  pl.BlockSpec((seq, width), lambda b, j, i: (b, j))],
        out_specs=pl.BlockSpec((tq, width), lambda b, j, i: (b * nq + i, j)),
        out_shape=jax.ShapeDtypeStruct((q.shape[0], SB_W), F32),
        scratch_shapes=[pltpu.VMEM((tq, width), F32), pltpu.VMEM((SB_STEP_HEADS, tq, LANES), F32)],
        compiler_params=pltpu.CompilerParams(
            dimension_semantics=("parallel", "parallel", "arbitrary"), vmem_limit_bytes=VMEM_LIMIT),
        name="sb_prompt",
    )(q, kb, vb)


def _sb_sample_kernel(q_ref, k_ref, v_ref, ck_ref, cv_ref, prompt_rows_ref, o_ref):
    del prompt_rows_ref
    ts = q_ref.shape[0]
    past = ck_ref.shape[0]
    tk = SB_BLOCK
    ustrict = _strict_lower_ones(tk)
    ustrict_d = ustrict[:LANES, :LANES]
    r = lax.broadcasted_iota(jnp.int32, (ts, LANES), 0)
    c = lax.broadcasted_iota(jnp.int32, (ts, LANES), 1)
    diag = c < r
    lane_head = lax.broadcasted_iota(jnp.int32, (1, LANES), 1) // HEAD_DIM
    pad = jnp.zeros((LANES - ts, LANES), BF16)
    heads = []
    for pr in range(q_ref.shape[1] // LANES):
        ln = slice(pr * LANES, (pr + 1) * LANES)
        for hh in range(2):
            hm = lane_head == hh
            heads.append((ln, hm, jnp.where(hm, q_ref[:, ln], jnp.zeros((ts, LANES), BF16))))

    def own(hm, v2):
        return jnp.where(hm, v2, jnp.zeros_like(v2))

    zero = jnp.zeros((ts, LANES), F32)
    results = _sb_blocks([(qh, jnp.concatenate([k_ref[:, ln], pad], axis=0),
                           own(hm, jnp.concatenate([v_ref[:, ln], pad], axis=0)), zero)
                          for ln, hm, qh in heads], ustrict_d, diag)
    accs = [out for out, _ in results]
    for kb in range(past // tk - 1, -1, -1):
        rows = slice(kb * tk, (kb + 1) * tk)
        results = _sb_blocks([(qh, ck_ref[rows, ln].astype(BF16), own(hm, cv_ref[rows, ln].astype(BF16)), carry)
                              for (ln, hm, qh), (_, carry) in zip(heads, results)], ustrict, None)
        accs = [acc + out for acc, (out, _) in zip(accs, results)]
    for pr in range(q_ref.shape[1] // LANES):
        o_ref[:, pr * LANES:(pr + 1) * LANES] = accs[2 * pr] + accs[2 * pr + 1]


def _sb_sample(q, kb, vb, cache_k, cache_v, o, n_prompt, batch, ts):
    past = cache_k.shape[1]
    off = n_prompt // ts
    new = pl.BlockSpec((ts, SB_W), lambda b: (off + b, 0))
    old = pl.BlockSpec((None, past, SB_W), lambda b: (b, 0, 0))
    return pl.pallas_call(
        _sb_sample_kernel,
        grid=(batch,),
        in_specs=[new, new, new, old, old, pl.BlockSpec(memory_space=pl.ANY)],
        out_specs=new,
        out_shape=jax.ShapeDtypeStruct(o.shape, o.dtype),
        input_output_aliases={5: 0},
        compiler_params=pltpu.CompilerParams(dimension_semantics=("parallel",),
                                             vmem_limit_bytes=VMEM_LIMIT),
        name="sb_sample",
    )(q, kb, vb, cache_k, cache_v, o)


def _mixer_out_kernel(h_ref, ua_ref, va_ref, o_ref, ws_ref, bs_ref, gav_ref, gao_ref, gsb_ref, wo_ref,
                      *refs, n_prompt_tiles):
    h1_ref, vas_ref = refs[-2:]
    tm = h_ref.shape[0]
    bd = _group_mean_matrix()
    lane_group = lax.broadcasted_iota(jnp.int32, (1, LANES), 1) // GROUP
    u = _gelu(ua_ref[...])
    vn = _group_rms(_gelu(va_ref[...]), gav_ref[...], bd)

    @pl.when(pl.program_id(0) >= n_prompt_tiles)
    def _():
        vas_ref[...] = vn

    vnb = vn.astype(BF16)
    rows = []
    for rr in range(tm // MIX_L):
        slabs = []
        for j in range(A_W // LANES):
            v2 = vnb[rr * MIX_L:(rr + 1) * MIX_L, j * LANES:(j + 1) * LANES]
            mix = jnp.zeros((MIX_L, LANES), F32)
            for gg in range(2):
                v2g = jnp.where(lane_group == gg, v2, jnp.zeros_like(v2))
                mix = mix + _dot(ws_ref[2 * j + gg], v2g)
            slabs.append(mix)
        rows.append(jnp.concatenate(slabs, axis=1) + bs_ref[...])
    mix = jnp.concatenate(rows, axis=0)
    oa = _group_rms(u * mix, gao_ref[...], bd)
    osb = _group_rms(o_ref[...], gsb_ref[...], bd)
    cat = jnp.concatenate([osb, oa], axis=1).astype(BF16)
    h1_ref[...] = h_ref[...] + _dot(cat, wo_ref[...])


def _mixer_out(h, ua, va, o, ws, bs, gav, gao, gsb, wo, n_prompt, layer, depth, stacked):
    n = h.shape[0]
    tm = TOK_TILE
    npt = n_prompt // tm
    row = lambda i: (i, 0)
    fixed = lambda i: (0, 0)
    variant = lambda i: jnp.where(i < npt, 0, 1)
    _, sample_blk = _split_rows_specs(layer, tm, A_W, npt)
    carried = [] if stacked is None else [stacked]
    return pl.pallas_call(
        functools.partial(_mixer_out_kernel, n_prompt_tiles=npt),
        grid=(n // tm,),
        in_specs=[pl.BlockSpec((tm, D_MODEL), row), pl.BlockSpec((tm, A_W), row),
                  pl.BlockSpec((tm, A_W), row), pl.BlockSpec((tm, SB_W), row),
                  pl.BlockSpec((None, 2 * A_W // LANES, MIX_L, MIX_L), lambda i: (variant(i), 0, 0, 0)),
                  pl.BlockSpec((None, MIX_L, A_W), lambda i: (variant(i), 0, 0)),
                  pl.BlockSpec((1, A_W), fixed), pl.BlockSpec((1, A_W), fixed),
                  pl.BlockSpec((1, SB_W), fixed), pl.BlockSpec((D_MODEL, D_MODEL), fixed)]
                 + [pl.BlockSpec(memory_space=pl.ANY)] * len(carried),
        out_specs=[pl.BlockSpec((tm, D_MODEL), row), sample_blk],
        out_shape=[jax.ShapeDtypeStruct((n, D_MODEL), F32),
                   jax.ShapeDtypeStruct((depth, n - n_prompt, A_W), F32)],
        input_output_aliases={10: 1} if carried else {},
        compiler_params=pltpu.CompilerParams(dimension_semantics=("arbitrary",),
                                             vmem_limit_bytes=VMEM_LIMIT),
        name="mixer_out",
    )(h, ua, va, o, ws, bs, gav, gao, gsb, wo, *carried)


def _spatial_params(w_s, b_s, ts):
    tril = jnp.tril(jnp.ones((MIX_L, MIX_L), bool))
    w_p = jnp.where(tril[None], w_s, 0.0)
    rep = MIX_L // ts
    w_small = jnp.where(tril[None, :ts, :ts], w_s[:, :ts, :ts], 0.0)
    eye = jnp.eye(rep, dtype=w_s.dtype)
    w_smp = jnp.einsum("ab,gts->gatbs", eye, w_small).reshape(-1, MIX_L, MIX_L)
    b_p = jnp.repeat(jnp.transpose(b_s), A_W // b_s.shape[0], axis=1)
    b_smp = jnp.tile(b_p[:ts], (rep, 1))
    return jnp.stack([w_p, w_smp]).astype(BF16), jnp.stack([b_p, b_smp])


def _argmax_tree(leaf, lo, hi):
    if hi - lo == 1:
        return leaf(lo)
    mid = (lo + hi) // 2
    va, ta = _argmax_tree(leaf, lo, mid)
    vb, tb = _argmax_tree(leaf, mid, hi)
    return jnp.maximum(va, vb), jnp.where(va >= vb, ta, tb)


def _vreg_rows(idx):
    if isinstance(idx, int):
        return slice(idx * SUBLANES, (idx + 1) * SUBLANES)
    return pl.ds(pl.multiple_of(idx * SUBLANES, SUBLANES), SUBLANES)


_PEER_PAIRS = [(i, j) for i in range(PEER_TOPK) for j in range(PEER_TOPK) if (i + 1) * (j + 1) <= PEER_TOPK]


def _peer_route_kernel(h_ref, g_ref, wq_ref, km_ref, xn_ref, e_ref, gate_ref,
                       s_ref, topv_ref, topi_ref, cv_ref, ce_ref, sc_ref, rese_ref, resg_ref):
    k_top = PEER_TOPK
    xn = _rms(h_ref[...], g_ref[...]).astype(BF16)
    xn_ref[...] = xn
    neg_inf = -jnp.inf
    none_yet = jnp.full((SUBLANES, LANES), -1, jnp.int32)

    def head(hd, _):
        pq = _dot(xn, wq_ref[hd]).astype(BF16)
        for c in range(SUBLANES):
            st = _dot_nt(km_ref[hd], pq[c * LANES:(c + 1) * LANES])
            s_ref[pl.ds(c, 2 * PEER_NKEYS, stride=SUBLANES), :] = st

        for p in range(2):
            def extract(k, prev, p=p):
                def leaf(n):
                    r = _vreg_rows(p * PEER_NKEYS + n)
                    v = jnp.where(prev == n, neg_inf, s_ref[r, :])
                    s_ref[r, :] = v
                    return v, n
                m, im = _argmax_tree(leaf, 0, PEER_NKEYS)
                topv_ref[_vreg_rows(p * k_top + k), :] = m
                topi_ref[_vreg_rows(p * k_top + k), :] = im
                return im
            lax.fori_loop(0, k_top, extract, none_yet)

        for idx, (i, j) in enumerate(_PEER_PAIRS):
            cv_ref[_vreg_rows(idx), :] = topv_ref[_vreg_rows(i), :] + topv_ref[_vreg_rows(k_top + j), :]
            ce_ref[_vreg_rows(idx), :] = (topi_ref[_vreg_rows(i), :] * PEER_NKEYS
                                          + topi_ref[_vreg_rows(k_top + j), :])

        def pick(k, prev):
            def leaf(idx):
                r = _vreg_rows(idx)
                eid = ce_ref[r, :]
                v = jnp.where(eid == prev, neg_inf, cv_ref[r, :])
                cv_ref[r, :] = v
                return v, eid
            m, em = _argmax_tree(leaf, 0, len(_PEER_PAIRS))
            sc_ref[_vreg_rows(k), :] = m
            rese_ref[_vreg_rows(hd * k_top + k), :] = em.astype(F32)
            return em
        lax.fori_loop(0, k_top, pick, none_yet)

        mx = sc_ref[_vreg_rows(0), :]
        exs = [jnp.exp(sc_ref[_vreg_rows(k), :] - mx) for k in range(k_top)]
        total = exs[0]
        for ex in exs[1:]:
            total = total + ex
        for k in range(k_top):
            resg_ref[_vreg_rows(hd * k_top + k), :] = exs[k] / total
        return 0

    lax.fori_loop(0, PEER_HEADS, head, 0)
    for c in range(SUBLANES):
        rows = slice(c * LANES, (c + 1) * LANES)
        e_ref[rows, :] = jnp.transpose(rese_ref[pl.ds(c, LANES, stride=SUBLANES), :]).astype(jnp.int32)
        gate_ref[rows, :] = jnp.transpose(resg_ref[pl.ds(c, LANES, stride=SUBLANES), :])


def _peer_route(h1, g, wq, km):
    n = h1.shape[0]
    tm = ROUTE_TILE
    row = lambda i: (i, 0)
    vregs = lambda count, dtype: pltpu.VMEM((count * SUBLANES, LANES), dtype)
    return pl.pallas_call(
        _peer_route_kernel,
        grid=(n // tm,),
        in_specs=[pl.BlockSpec((tm, D_MODEL), row), pl.BlockSpec((1, D_MODEL), lambda i: (0, 0)),
                  pl.BlockSpec((PEER_HEADS, D_MODEL, LANES), lambda i: (0, 0, 0)),
                  pl.BlockSpec((PEER_HEADS, 2 * PEER_NKEYS, LANES), lambda i: (0, 0, 0))],
        out_specs=[pl.BlockSpec((tm, D_MODEL), row), pl.BlockSpec((tm, LANES), row),
                   pl.BlockSpec((tm, LANES), row)],
        out_shape=[jax.ShapeDtypeStruct((n, D_MODEL), BF16),
                   jax.ShapeDtypeStruct((n, LANES), jnp.int32),
                   jax.ShapeDtypeStruct((n, LANES), F32)],
        scratch_shapes=[vregs(2 * PEER_NKEYS, F32), vregs(2 * PEER_TOPK, F32), vregs(2 * PEER_TOPK, jnp.int32),
                        vregs(len(_PEER_PAIRS), F32), vregs(len(_PEER_PAIRS), jnp.int32),
                        vregs(PEER_TOPK, F32), vregs(PEER_HEADS * PEER_TOPK, F32),
                        vregs(PEER_HEADS * PEER_TOPK, F32)],
        compiler_params=pltpu.CompilerParams(dimension_semantics=("parallel",),
                                             vmem_limit_bytes=VMEM_LIMIT),
        name="peer_route",
    )(h1, g, wq, km)


def _peer_key_matrix(sub_keys):
    z = jnp.zeros_like(sub_keys[:, 0])
    k0 = jnp.concatenate([sub_keys[:, 0], z], axis=2)
    k1 = jnp.concatenate([z, sub_keys[:, 1]], axis=2)
    return jnp.concatenate([k0, k1], axis=1).astype(BF16)


def _peer_dense_kernel(xn_ref, e_ref, g_ref, u_ref, v_ref, o_ref, gs_ref):
    tt = xn_ref.shape[0]
    ce = u_ref.shape[0]
    stride = tt + GS_PAD
    c = pl.program_id(1)

    @pl.when(c == 0)
    def _():
        o_ref[...] = jnp.zeros_like(o_ref)
        sub = lax.broadcasted_iota(jnp.int32, (PEER_NKEYS, LANES), 0)

        def tok(t, _):
            e_row = e_ref[pl.ds(t, 1), :]
            g_row = g_ref[pl.ds(t, 1), :]
            i1 = e_row // PEER_NKEYS
            i2 = e_row % PEER_NKEYS
            left = jnp.where(sub == i1, g_row, 0.0).astype(BF16)
            right = jnp.where(sub == i2, 1.0, 0.0).astype(BF16)
            gs_ref[pl.ds(t, PEER_NKEYS, stride=stride), :] = _dot_nt(left, right)
            return 0

        lax.fori_loop(0, tt, tok, 0, unroll=BUILD_UNROLL)

    hid = _dot_nt(xn_ref[...], u_ref[...])
    slabs = []
    for al in range(ce // LANES):
        start = pl.multiple_of((c * (ce // LANES) + al) * stride, 8)
        slabs.append(gs_ref[pl.ds(start, tt), :])
    gate = jnp.concatenate(slabs, axis=1)
    a = (gate * _gelu(hid)).astype(BF16)
    o_ref[...] += _dot(a, v_ref[...])


def _peer_dense(xn, e, g, u, v):
    n = xn.shape[0]
    tt = PEER_TILE
    ce = PEER_CHUNK
    row = lambda i, c: (i, 0)
    chunk = lambda i, c: (c, 0)
    return pl.pallas_call(
        _peer_dense_kernel,
        grid=(n // tt, PEER_EXPERTS // ce),
        in_specs=[pl.BlockSpec((tt, D_MODEL), row), pl.BlockSpec((tt, LANES), row),
                  pl.BlockSpec((tt, LANES), row), pl.BlockSpec((ce, D_MODEL), chunk),
                  pl.BlockSpec((ce, D_MODEL), chunk)],
        out_specs=pl.BlockSpec((tt, D_MODEL), row),
        out_shape=jax.ShapeDtypeStruct((n, D_MODEL), F32),
        scratch_shapes=[pltpu.VMEM((PEER_NKEYS * (tt + GS_PAD), LANES), F32)],
        compiler_params=pltpu.CompilerParams(dimension_semantics=("parallel", "arbitrary"),
                                             vmem_limit_bytes=PEER_VMEM_LIMIT),
        name="peer_dense",
    )(xn, e, g, u, v)


def _ple_kernel(h_ref, peer_ref, p_ref, g_ref, wg_ref, bg_ref, wp_ref, gf_ref, *o_refs, n_prompt_tiles):
    h = h_ref[...] + peer_ref[...]
    xn = _rms(h, g_ref[...]).astype(BF16)
    gate = jax.nn.sigmoid(_dot(xn, wg_ref[...]) + bg_ref[...])
    out = h + _dot(p_ref[...].astype(BF16), wp_ref[...]) * gate
    if len(o_refs) == 1:
        o_refs[0][...] = out
        return
    y = _rms(out, gf_ref[...])
    is_prompt = pl.program_id(0) < n_prompt_tiles

    @pl.when(is_prompt)
    def _():
        o_refs[0][...] = y

    @pl.when(jnp.logical_not(is_prompt))
    def _():
        o_refs[1][...] = y


def _ple(h1, peer_out, p, g, wg, bg, wp, gf, n_prompt, final):
    n = h1.shape[0]
    tm = TOK_TILE
    npt = n_prompt // tm
    row = lambda i: (i, 0)
    fixed = lambda i: (0, 0)
    if final:
        out_specs = [pl.BlockSpec((tm, D_MODEL), lambda i: (jnp.minimum(i, npt - 1), 0)),
                     pl.BlockSpec((tm, D_MODEL), lambda i: (jnp.maximum(i - npt, 0), 0))]
        out_shape = [jax.ShapeDtypeStruct((n_prompt, D_MODEL), F32),
                     jax.ShapeDtypeStruct((n - n_prompt, D_MODEL), F32)]
    else:
        out_specs = [pl.BlockSpec((tm, D_MODEL), row)]
        out_shape = [jax.ShapeDtypeStruct((n, D_MODEL), F32)]
    return pl.pallas_call(
        functools.partial(_ple_kernel, n_prompt_tiles=npt),
        grid=(n // tm,),
        in_specs=[pl.BlockSpec((tm, D_MODEL), row), pl.BlockSpec((tm, D_MODEL), row),
                  pl.BlockSpec((tm, PLE_DIM), row),
                  pl.BlockSpec((1, D_MODEL), fixed), pl.BlockSpec((D_MODEL, D_MODEL), fixed),
                  pl.BlockSpec((1, D_MODEL), fixed), pl.BlockSpec((PLE_DIM, D_MODEL), fixed),
                  pl.BlockSpec((1, D_MODEL), fixed)],
        out_specs=out_specs,
        out_shape=out_shape,
        compiler_params=pltpu.CompilerParams(dimension_semantics=("arbitrary",),
                                             vmem_limit_bytes=VMEM_LIMIT),
        name="ple",
    )(h1, peer_out, p, g, wg, bg, wp, gf)


def kernel(x_prompt, x_sample, cache_k, cache_v, p_prompt, p_sample, g_mix, w_in, g_sb_out, w_spatial, b_spatial, g_a_v, g_a_out, w_out, g_ffn, w_peer_q, peer_sub_keys, peer_u, peer_v, g_ple, w_ple_gate, b_ple_gate, w_ple, g_final):
    depth = w_in.shape[0]
    b, t, d = x_prompt.shape
    bs, ts, _ = x_sample.shape
    past = cache_k.shape[2]
    n_p, n_s = b * t, bs * ts
    assert d == D_MODEL and t % SB_BLOCK == 0 and past % SB_BLOCK == 0
    assert n_p % TOK_TILE == 0 and n_s % TOK_TILE == 0 and MIX_L % ts == 0 and n_p % ts == 0
    assert (n_p + n_s) % ROUTE_TILE == 0 and (n_p + n_s) % PEER_TILE == 0

    h = jnp.concatenate([x_prompt.reshape(n_p, d), x_sample.reshape(n_s, d)], axis=0)
    row = lambda a: a.reshape(1, -1)
    kv = None
    vas = None
    for l in range(depth):
        (q, kb, vb, ua, va), kv = _inproj(h, row(g_mix[l]), w_in[l].astype(BF16), l, depth, n_p, kv)
        o = _sb_prompt(q, kb, vb, b, t)
        o = _sb_sample(q, kb, vb, cache_k[l].reshape(bs, past, SB_W), cache_v[l].reshape(bs, past, SB_W),
                       o, n_p, bs, ts)
        ws, bsp = _spatial_params(w_spatial[l], b_spatial[l], ts)
        h1, vas = _mixer_out(h, ua, va, o, ws, bsp, row(g_a_v[l]), row(g_a_out[l]), row(g_sb_out[l]),
                             w_out[l].astype(BF16), n_p, l, depth, vas)
        wq = jnp.transpose(w_peer_q[l].reshape(d, PEER_HEADS, LANES), (1, 0, 2)).astype(BF16)
        xn, e, gate = _peer_route(h1, row(g_ffn[l]), wq, _peer_key_matrix(peer_sub_keys[l]))
        peer_out = _peer_dense(xn, e, gate, peer_u[l].astype(BF16), peer_v[l].astype(BF16))
        p_l = jnp.concatenate([p_prompt[l].reshape(n_p, PLE_DIM), p_sample[l].reshape(n_s, PLE_DIM)], axis=0)
        res = _ple(h1, peer_out, p_l, row(g_ple[l]), w_ple_gate[l].astype(BF16), row(b_ple_gate[l]),
                   w_ple[l].astype(BF16), row(g_final), n_p, final=(l == depth - 1))
        h = res[0]
    y_p, y_s = res
    k_p, v_p, k_s, v_s = kv
    return (y_p.reshape(b, t, d), y_s.reshape(bs, ts, d),
            k_p.reshape(depth, b, t, N_HEADS, HEAD_DIM), v_p.reshape(depth, b, t, N_HEADS, HEAD_DIM),
            k_s.reshape(depth, bs, ts, N_HEADS, HEAD_DIM), v_s.reshape(depth, bs, ts, N_HEADS, HEAD_DIM),
            vas.reshape(depth, bs, ts, A_W // GROUP, GROUP))
```

```python
import functools

import jax
import jax.numpy as jnp
from jax import lax
from jax.experimental import pallas as pl
from jax.experimental.pallas import tpu as pltpu

F32 = jnp.float32
BF16 = jnp.bfloat16

LANES = 128
SUBLANES = 8
D_MODEL = 1024
N_HEADS = 8
HEAD_DIM = 64
SB_W = N_HEADS * HEAD_DIM
A_W = 512
IN_W = 3 * SB_W + 2 * A_W
GROUP = 64
MIX_L = 128
PEER_HEADS = 8
PEER_NKEYS = 128
PEER_TOPK = 16
PEER_EXPERTS = PEER_NKEYS * PEER_NKEYS
PLE_DIM = 256
RMS_EPS = 1e-6
TOK_TILE = 512
ROUTE_TILE = SUBLANES * LANES
SB_BLOCK = 256
SB_STEP_HEADS = 8
PEER_TILE = 512
PEER_CHUNK = 1024
PEER_VMEM_LIMIT = 58 * 1024 * 1024
BUILD_UNROLL = 32
GS_PAD = 8
VMEM_LIMIT = 48 * 1024 * 1024
INV_SQRT2 = 0.7071067811865476


def _dot(a, b):
    return jnp.dot(a, b, preferred_element_type=F32)


def _dot_nt(a, b):
    return lax.dot_general(a, b, (((1,), (1,)), ((), ())), preferred_element_type=F32)


def _split_bf16(x):
    hi = x.astype(BF16)
    lo = (x - hi.astype(F32)).astype(BF16)
    return hi, lo


def _gelu(x):
    return 0.5 * x * (1.0 + lax.erf(x * INV_SQRT2))


def _rms(x, g):
    ms = jnp.mean(x * x, axis=-1, keepdims=True)
    return x * lax.rsqrt(ms + RMS_EPS) * g


def _group_mean_matrix():
    r = lax.broadcasted_iota(jnp.int32, (LANES, LANES), 0) // GROUP
    c = lax.broadcasted_iota(jnp.int32, (LANES, LANES), 1) // GROUP
    return jnp.where(r == c, 1.0 / GROUP, 0.0).astype(BF16)


def _group_rms(x, g, bd):
    xx = x * x
    hi, lo = _split_bf16(xx)
    parts = []
    for j in range(x.shape[1] // LANES):
        sl = slice(LANES * j, LANES * (j + 1))
        parts.append(_dot(hi[:, sl], bd) + _dot(lo[:, sl], bd))
    ms = jnp.concatenate(parts, axis=1)
    return x * lax.rsqrt(ms + RMS_EPS) * g


def _inproj_kernel(x_ref, g_ref, w_ref, wkv_t_ref, *refs, n_prompt_tiles):
    q_ref, ktb_ref, vtb_ref, ua_ref, va_ref, kp_ref, vp_ref, ks_ref, vs_ref = refs[-9:]
    xn = _rms(x_ref[...], g_ref[...]).astype(BF16)
    q = _dot(xn, w_ref[:, 0:SB_W])
    q_ref[...] = (q * (HEAD_DIM ** -0.5)).astype(BF16)
    ua_ref[...] = _dot(xn, w_ref[:, 3 * SB_W:3 * SB_W + A_W])
    va_ref[...] = _dot(xn, w_ref[:, 3 * SB_W + A_W:IN_W])
    kt = _dot_nt(wkv_t_ref[0:SB_W, :], xn)
    vt = _dot_nt(wkv_t_ref[SB_W:2 * SB_W, :], xn)
    for j in range(ktb_ref.shape[0]):
        ktb_ref[j] = kt[:, j * SB_BLOCK:(j + 1) * SB_BLOCK].astype(BF16)
        vtb_ref[j] = vt[:, j * SB_BLOCK:(j + 1) * SB_BLOCK].astype(BF16)
    is_prompt = pl.program_id(0) < n_prompt_tiles

    @pl.when(is_prompt)
    def _():
        kp_ref[...] = kt
        vp_ref[...] = vt

    @pl.when(jnp.logical_not(is_prompt))
    def _():
        ks_ref[...] = _dot(xn, w_ref[:, SB_W:2 * SB_W])
        vs_ref[...] = _dot(xn, w_ref[:, 2 * SB_W:3 * SB_W])


def _split_rows_specs(layer, tile, width, n_prompt_tiles):
    prompt = pl.BlockSpec((None, tile, width), lambda i: (layer, jnp.minimum(i, n_prompt_tiles - 1), 0))
    sample = pl.BlockSpec((None, tile, width), lambda i: (layer, jnp.maximum(i - n_prompt_tiles, 0), 0))
    return prompt, sample


def _inproj(h, g, w, wkv_t, layer, depth, batch, seq, stacked):
    n = h.shape[0]
    tm = TOK_TILE
    n_prompt = batch * seq
    npt, per_seq = n_prompt // tm, seq // tm
    row = lambda i: (i, 0)
    fixed = lambda i: (0, 0)
    out_f = jax.ShapeDtypeStruct((n, SB_W), F32)
    out_b = jax.ShapeDtypeStruct((n, SB_W), BF16)
    out_t = jax.ShapeDtypeStruct((n // SB_BLOCK, SB_W, SB_BLOCK), BF16)
    stack_p = jax.ShapeDtypeStruct((depth, batch, SB_W, seq), F32)
    stack_s = jax.ShapeDtypeStruct((depth, n - n_prompt, SB_W), F32)
    blk = pl.BlockSpec((tm, SB_W), row)
    blk_t = pl.BlockSpec((tm // SB_BLOCK, SB_W, SB_BLOCK), lambda i: (i, 0, 0))
    _, sample_blk = _split_rows_specs(layer, tm, SB_W, npt)

    def prompt_idx(i):
        t = jnp.minimum(i, npt - 1)
        return (layer, t // per_seq, 0, t % per_seq)

    prompt_blk = pl.BlockSpec((None, None, SB_W, tm), prompt_idx)
    carried = [] if stacked is None else list(stacked)
    outs = pl.pallas_call(
        functools.partial(_inproj_kernel, n_prompt_tiles=npt),
        grid=(n // tm,),
        in_specs=[pl.BlockSpec((tm, D_MODEL), row), pl.BlockSpec((1, D_MODEL), fixed),
                  pl.BlockSpec((D_MODEL, IN_W), fixed), pl.BlockSpec((2 * SB_W, D_MODEL), fixed)]
                 + [pl.BlockSpec(memory_space=pl.ANY)] * len(carried),
        out_specs=[blk, blk_t, blk_t, blk, blk, prompt_blk, prompt_blk, sample_blk, sample_blk],
        out_shape=[out_b, out_t, out_t, out_f, out_f, stack_p, stack_p, stack_s, stack_s],
        input_output_aliases={4 + j: 5 + j for j in range(len(carried))},
        compiler_params=pltpu.CompilerParams(dimension_semantics=("arbitrary",),
                                             vmem_limit_bytes=VMEM_LIMIT),
        name="inproj",
    )(h, g, w, wkv_t, *carried)
    return outs[:5], tuple(outs[5:])


def _strict_lower_ones(n):
    r = lax.broadcasted_iota(jnp.int32, (n, n), 0)
    c = lax.broadcasted_iota(jnp.int32, (n, n), 1)
    return jnp.where(r > c, 1.0, 0.0).astype(BF16)


def _sb_blocks(chains, ustrict, mask):
    zs = [_dot(qh, kt) for qh, kt, _, _ in chains]
    log_betas, log_1ms, splits = [], [], []
    for z in zs:
        sp = jnp.log(1.0 + jnp.exp(-jnp.abs(z)))
        log_beta = jnp.minimum(z, 0.0) - sp
        log_1m = log_beta - z
        if mask is not None:
            log_1m = jnp.where(mask, log_1m, 0.0)
        log_betas.append(log_beta)
        log_1ms.append(log_1m)
        splits.append(_split_bf16(log_1m))
    laters = [_dot(hi, ustrict) + _dot(lo, ustrict) for hi, lo in splits]
    ws, carries = [], []
    for (_, kt, _, carry), log_beta, log_1m, later in zip(chains, log_betas, log_1ms, laters):
        rest = later + jnp.concatenate([carry] * (kt.shape[1] // LANES), axis=1)
        w = jnp.exp(log_beta + rest)
        if mask is not None:
            w = jnp.where(mask, w, 0.0)
        ws.append(w.astype(BF16))
        carries.append(carry + jnp.broadcast_to(later[:, :1] + log_1m[:, :1], carry.shape))
    return [(_dot_nt(w, vth), carry) for w, (_, _, vth, _), carry in zip(ws, chains, carries)]


def _sb_prompt_kernel(q_ref, kt_ref, vt_ref, o_ref, acc_ref, carry_ref):
    tq = q_ref.shape[0]
    tk = tq
    qi = pl.program_id(2)
    ustrict = _strict_lower_ones(tk)
    r = lax.broadcasted_iota(jnp.int32, (tq, tk), 0)
    c = lax.broadcasted_iota(jnp.int32, (tq, tk), 1)
    diag = c < r
    lane_head = lax.broadcasted_iota(jnp.int32, (1, LANES), 1) // HEAD_DIM
    row_head = lax.broadcasted_iota(jnp.int32, (LANES, 1), 0) // HEAD_DIM
    n_pairs = q_ref.shape[1] // LANES
    lanes = [slice(pr * LANES, (pr + 1) * LANES) for pr in range(n_pairs)]
    qhs = [[jnp.where(lane_head == hh, q_ref[:, ln], jnp.zeros((tq, LANES), BF16)) for hh in range(2)]
           for ln in lanes]
    acc_ref[...] = jnp.zeros_like(acc_ref)
    carry_ref[...] = jnp.zeros_like(carry_ref)

    def step(kb, mask):
        chains = []
        for pr, ln in enumerate(lanes):
            kt = kt_ref[kb, ln, :]
            vt = vt_ref[kb, ln, :]
            for hh in range(2):
                vth = jnp.where(row_head == hh, vt, jnp.zeros_like(vt))
                chains.append((qhs[pr][hh], kt, vth, carry_ref[2 * pr + hh]))
        results = _sb_blocks(chains, ustrict, mask)
        for i, (_, carry) in enumerate(results):
            carry_ref[i] = carry
        for pr, ln in enumerate(lanes):
            acc_ref[:, ln] += results[2 * pr][0] + results[2 * pr + 1][0]

    step(qi, diag)

    def body(it, _):
        step(qi - 1 - it, None)
        return 0

    lax.fori_loop(0, qi, body, 0)
    o_ref[...] = acc_ref[...]


def _sb_prompt(q, ktb, vtb, batch, seq):
    tq = SB_BLOCK
    nq = seq // tq
    width = SB_STEP_HEADS * HEAD_DIM
    return pl.pallas_call(
        _sb_prompt_kernel,
        grid=(batch, SB_W // width, nq),
        in_specs=[pl.BlockSpec((tq, width), lambda b, j, i: (b * nq + i, j)),
                  pl.BlockSpec((nq, width, tq), lambda b, j, i: (b, j, 0)),
                  pl.BlockSpec((nq, width, tq), lambda b, j, i: (b, j, 0))],
        out_specs=pl.BlockSpec((tq, width), lambda b, j, i: (b * nq + i, j)),
        out_shape=jax.ShapeDtypeStruct((q.shape[0], SB_W), F32),
        scratch_shapes=[pltpu.VMEM((tq, width), F32), pltpu.VMEM((SB_STEP_HEADS, tq, LANES), F32)],
        compiler_params=pltpu.CompilerParams(
            dimension_semantics=("parallel", "parallel", "arbitrary"), vmem_limit_bytes=VMEM_LIMIT),
        name="sb_prompt",
    )(q, ktb, vtb)


def _sb_sample_kernel(q_ref, k_ref, v_ref, ck_ref, cv_ref, prompt_rows_ref, o_ref):
    del prompt_rows_ref
    ts = q_ref.shape[0]
    past = ck_ref.shape[1]
    tk = SB_BLOCK
    ustrict = _strict_lower_ones(tk)
    ustrict_d = ustrict[:LANES, :LANES]
    r = lax.broadcasted_iota(jnp.int32, (ts, LANES), 0)
    c = lax.broadcasted_iota(jnp.int32, (ts, LANES), 1)
    diag = c < r
    lane_head = lax.broadcasted_iota(jnp.int32, (1, LANES), 1) // HEAD_DIM
    row_head = lax.broadcasted_iota(jnp.int32, (LANES, 1), 0) // HEAD_DIM
    pad = jnp.zeros((LANES - ts, LANES), F32)
    n_pairs = q_ref.shape[1] // LANES
    lanes = [slice(pr * LANES, (pr + 1) * LANES) for pr in range(n_pairs)]
    heads = [(pr, hh, jnp.where(lane_head == hh, q_ref[:, lanes[pr]], jnp.zeros((ts, LANES), BF16)))
             for pr in range(n_pairs) for hh in range(2)]

    def own(hh, vt):
        return jnp.where(row_head == hh, vt, jnp.zeros_like(vt))

    def new_t(ref, pr):
        return jnp.transpose(jnp.concatenate([ref[:, lanes[pr]], pad], axis=0)).astype(BF16)

    zero = jnp.zeros((ts, LANES), F32)
    new_k = [new_t(k_ref, pr) for pr in range(n_pairs)]
    new_v = [new_t(v_ref, pr) for pr in range(n_pairs)]
    results = _sb_blocks([(qh, new_k[pr], own(hh, new_v[pr]), zero) for pr, hh, qh in heads], ustrict_d, diag)
    accs = [out for out, _ in results]
    for kb in range(past // tk - 1, -1, -1):
        keys = slice(kb * tk, (kb + 1) * tk)
        old_k = [ck_ref[lanes[pr], keys].astype(BF16) for pr in range(n_pairs)]
        old_v = [cv_ref[lanes[pr], keys].astype(BF16) for pr in range(n_pairs)]
        results = _sb_blocks([(qh, old_k[pr], own(hh, old_v[pr]), carry)
                              for (pr, hh, qh), (_, carry) in zip(heads, results)], ustrict, None)
        accs = [acc + out for acc, (out, _) in zip(accs, results)]
    for pr in range(n_pairs):
        o_ref[:, lanes[pr]] = accs[2 * pr] + accs[2 * pr + 1]


def _sb_sample(q, k_s, v_s, cache_kt, cache_vt, o, layer, n_prompt, ts):
    _, batch, width, past = cache_kt.shape
    off = n_prompt // ts
    rows = pl.BlockSpec((ts, SB_W), lambda b: (off + b, 0))
    new = pl.BlockSpec((None, ts, SB_W), lambda b: (layer, b, 0))
    old = pl.BlockSpec((None, None, width, past), lambda b: (layer, b, 0, 0))
    return pl.pallas_call(
        _sb_sample_kernel,
        grid=(batch,),
        in_specs=[rows, new, new, old, old, pl.BlockSpec(memory_space=pl.ANY)],
        out_specs=rows,
        out_shape=jax.ShapeDtypeStruct(o.shape, o.dtype),
        input_output_aliases={5: 0},
        compiler_params=pltpu.CompilerParams(dimension_semantics=("parallel",),
                                             vmem_limit_bytes=VMEM_LIMIT),
        name="sb_sample",
    )(q, k_s, v_s, cache_kt, cache_vt, o)


def _mixer_out_kernel(h_ref, ua_ref, va_ref, o_ref, ws_ref, bs_ref, gav_ref, gao_ref, gsb_ref, wo_ref,
                      *refs, n_prompt_tiles):
    h1_ref, vas_ref = refs[-2:]
    tm = h_ref.shape[0]
    bd = _group_mean_matrix()
    lane_group = lax.broadcasted_iota(jnp.int32, (1, LANES), 1) // GROUP
    u = _gelu(ua_ref[...])
    vn = _group_rms(_gelu(va_ref[...]), gav_ref[...], bd)

    @pl.when(pl.program_id(0) >= n_prompt_tiles)
    def _():
        vas_ref[...] = vn

    vnb = vn.astype(BF16)
    rows = []
    for rr in range(tm // MIX_L):
        slabs = []
        for j in range(A_W // LANES):
            v2 = vnb[rr * MIX_L:(rr + 1) * MIX_L, j * LANES:(j + 1) * LANES]
            mix = jnp.zeros((MIX_L, LANES), F32)
            for gg in range(2):
                v2g = jnp.where(lane_group == gg, v2, jnp.zeros_like(v2))
                mix = mix + _dot(ws_ref[2 * j + gg], v2g)
            slabs.append(mix)
        rows.append(jnp.concatenate(slabs, axis=1) + bs_ref[...])
    mix = jnp.concatenate(rows, axis=0)
    oa = _group_rms(u * mix, gao_ref[...], bd)
    osb = _group_rms(o_ref[...], gsb_ref[...], bd)
    cat = jnp.concatenate([osb, oa], axis=1).astype(BF16)
    h1_ref[...] = h_ref[...] + _dot(cat, wo_ref[...])


def _mixer_out(h, ua, va, o, ws, bs, gav, gao, gsb, wo, n_prompt, layer, depth, stacked):
    n = h.shape[0]
    tm = TOK_TILE
    npt = n_prompt // tm
    row = lambda i: (i, 0)
    fixed = lambda i: (0, 0)
    variant = lambda i: jnp.where(i < npt, 0, 1)
    _, sample_blk = _split_rows_specs(layer, tm, A_W, npt)
    carried = [] if stacked is None else [stacked]
    return pl.pallas_call(
        functools.partial(_mixer_out_kernel, n_prompt_tiles=npt),
        grid=(n // tm,),
        in_specs=[pl.BlockSpec((tm, D_MODEL), row), pl.BlockSpec((tm, A_W), row),
                  pl.BlockSpec((tm, A_W), row), pl.BlockSpec((tm, SB_W), row),
                  pl.BlockSpec((None, 2 * A_W // LANES, MIX_L, MIX_L), lambda i: (variant(i), 0, 0, 0)),
                  pl.BlockSpec((None, MIX_L, A_W), lambda i: (variant(i), 0, 0)),
                  pl.BlockSpec((1, A_W), fixed), pl.BlockSpec((1, A_W), fixed),
                  pl.BlockSpec((1, SB_W), fixed), pl.BlockSpec((D_MODEL, D_MODEL), fixed)]
                 + [pl.BlockSpec(memory_space=pl.ANY)] * len(carried),
        out_specs=[pl.BlockSpec((tm, D_MODEL), row), sample_blk],
        out_shape=[jax.ShapeDtypeStruct((n, D_MODEL), F32),
                   jax.ShapeDtypeStruct((depth, n - n_prompt, A_W), F32)],
        input_output_aliases={10: 1} if carried else {},
        compiler_params=pltpu.CompilerParams(dimension_semantics=("arbitrary",),
                                             vmem_limit_bytes=VMEM_LIMIT),
        name="mixer_out",
    )(h, ua, va, o, ws, bs, gav, gao, gsb, wo, *carried)


def _spatial_params(w_s, b_s, ts):
    tril = jnp.tril(jnp.ones((MIX_L, MIX_L), bool))
    w_p = jnp.where(tril[None], w_s, 0.0)
    rep = MIX_L // ts
    w_small = jnp.where(tril[None, :ts, :ts], w_s[:, :ts, :ts], 0.0)
    eye = jnp.eye(rep, dtype=w_s.dtype)
    w_smp = jnp.einsum("ab,gts->gatbs", eye, w_small).reshape(-1, MIX_L, MIX_L)
    b_p = jnp.repeat(jnp.transpose(b_s), A_W // b_s.shape[0], axis=1)
    b_smp = jnp.tile(b_p[:ts], (rep, 1))
    return jnp.stack([w_p, w_smp]).astype(BF16), jnp.stack([b_p, b_smp])


def _argmax_tree(leaf, lo, hi):
    if hi - lo == 1:
        return leaf(lo)
    mid = (lo + hi) // 2
    va, ta = _argmax_tree(leaf, lo, mid)
    vb, tb = _argmax_tree(leaf, mid, hi)
    return jnp.maximum(va, vb), jnp.where(va >= vb, ta, tb)


def _vreg_rows(idx):
    if isinstance(idx, int):
        return slice(idx * SUBLANES, (idx + 1) * SUBLANES)
    return pl.ds(pl.multiple_of(idx * SUBLANES, SUBLANES), SUBLANES)


_PEER_PAIRS = [(i, j) for i in range(PEER_TOPK) for j in range(PEER_TOPK) if (i + 1) * (j + 1) <= PEER_TOPK]


def _peer_route_kernel(h_ref, g_ref, wq_ref, km_ref, xn_ref, e_ref, gate_ref,
                       s_ref, topv_ref, topi_ref, cv_ref, ce_ref, sc_ref, rese_ref, resg_ref):
    k_top = PEER_TOPK
    xn = _rms(h_ref[...], g_ref[...]).astype(BF16)
    xn_ref[...] = xn
    neg_inf = -jnp.inf
    none_yet = jnp.full((SUBLANES, LANES), -1, jnp.int32)

    def head(hd, _):
        pq = _dot(xn, wq_ref[hd]).astype(BF16)
        for c in range(SUBLANES):
            st = _dot_nt(km_ref[hd], pq[c * LANES:(c + 1) * LANES])
            s_ref[pl.ds(c, 2 * PEER_NKEYS, stride=SUBLANES), :] = st

        for p in range(2):
            def extract(k, prev, p=p):
                def leaf(n):
                    r = _vreg_rows(p * PEER_NKEYS + n)
                    v = jnp.where(prev == n, neg_inf, s_ref[r, :])
                    s_ref[r, :] = v
                    return v, n
                m, im = _argmax_tree(leaf, 0, PEER_NKEYS)
                topv_ref[_vreg_rows(p * k_top + k), :] = m
                topi_ref[_vreg_rows(p * k_top + k), :] = im
                return im
            lax.fori_loop(0, k_top, extract, none_yet)

        for idx, (i, j) in enumerate(_PEER_PAIRS):
            cv_ref[_vreg_rows(idx), :] = topv_ref[_vreg_rows(i), :] + topv_ref[_vreg_rows(k_top + j), :]
            ce_ref[_vreg_rows(idx), :] = (topi_ref[_vreg_rows(i), :] * PEER_NKEYS
                                          + topi_ref[_vreg_rows(k_top + j), :])

        def pick(k, prev):
            def leaf(idx):
                r = _vreg_rows(idx)
                eid = ce_ref[r, :]
                v = jnp.where(eid == prev, neg_inf, cv_ref[r, :])
                cv_ref[r, :] = v
                return v, eid
            m, em = _argmax_tree(leaf, 0, len(_PEER_PAIRS))
            sc_ref[_vreg_rows(k), :] = m
            rese_ref[_vreg_rows(hd * k_top + k), :] = em.astype(F32)
            return em
        lax.fori_loop(0, k_top, pick, none_yet)

        mx = sc_ref[_vreg_rows(0), :]
        exs = [jnp.exp(sc_ref[_vreg_rows(k), :] - mx) for k in range(k_top)]
        total = exs[0]
        for ex in exs[1:]:
            total = total + ex
        for k in range(k_top):
            resg_ref[_vreg_rows(hd * k_top + k), :] = exs[k] / total
        return 0

    lax.fori_loop(0, PEER_HEADS, head, 0)
    for c in range(SUBLANES):
        rows = slice(c * LANES, (c + 1) * LANES)
        e_ref[rows, :] = jnp.transpose(rese_ref[pl.ds(c, LANES, stride=SUBLANES), :]).astype(jnp.int32)
        gate_ref[rows, :] = jnp.transpose(resg_ref[pl.ds(c, LANES, stride=SUBLANES), :])


def _peer_route(h1, g, wq, km):
    n = h1.shape[0]
    tm = ROUTE_TILE
    row = lambda i: (i, 0)
    vregs = lambda count, dtype: pltpu.VMEM((count * SUBLANES, LANES), dtype)
    return pl.pallas_call(
        _peer_route_kernel,
        grid=(n // tm,),
        in_specs=[pl.BlockSpec((tm, D_MODEL), row), pl.BlockSpec((1, D_MODEL), lambda i: (0, 0)),
                  pl.BlockSpec((PEER_HEADS, D_MODEL, LANES), lambda i: (0, 0, 0)),
                  pl.BlockSpec((PEER_HEADS, 2 * PEER_NKEYS, LANES), lambda i: (0, 0, 0))],
        out_specs=[pl.BlockSpec((tm, D_MODEL), row), pl.BlockSpec((tm, LANES), row),
                   pl.BlockSpec((tm, LANES), row)],
        out_shape=[jax.ShapeDtypeStruct((n, D_MODEL), BF16),
                   jax.ShapeDtypeStruct((n, LANES), jnp.int32),
                   jax.ShapeDtypeStruct((n, LANES), F32)],
        scratch_shapes=[vregs(2 * PEER_NKEYS, F32), vregs(2 * PEER_TOPK, F32), vregs(2 * PEER_TOPK, jnp.int32),
                        vregs(len(_PEER_PAIRS), F32), vregs(len(_PEER_PAIRS), jnp.int32),
                        vregs(PEER_TOPK, F32), vregs(PEER_HEADS * PEER_TOPK, F32),
                        vregs(PEER_HEADS * PEER_TOPK, F32)],
        compiler_params=pltpu.CompilerParams(dimension_semantics=("parallel",),
                                             vmem_limit_bytes=VMEM_LIMIT),
        name="peer_route",
    )(h1, g, wq, km)


def _peer_key_matrix(sub_keys):
    z = jnp.zeros_like(sub_keys[:, 0])
    k0 = jnp.concatenate([sub_keys[:, 0], z], axis=2)
    k1 = jnp.concatenate([z, sub_keys[:, 1]], axis=2)
    return jnp.concatenate([k0, k1], axis=1).astype(BF16)


def _peer_dense_kernel(xn_ref, e_ref, g_ref, u_ref, v_ref, o_ref, gs_ref):
    tt = xn_ref.shape[0]
    ce = u_ref.shape[0]
    stride = tt + GS_PAD
    c = pl.program_id(1)

    @pl.when(c == 0)
    def _():
        o_ref[...] = jnp.zeros_like(o_ref)
        sub = lax.broadcasted_iota(jnp.int32, (PEER_NKEYS, LANES), 0)

        def tok(t, _):
            e_row = e_ref[pl.ds(t, 1), :]
            g_row = g_ref[pl.ds(t, 1), :]
            i1 = e_row // PEER_NKEYS
            i2 = e_row % PEER_NKEYS
            left = jnp.where(sub == i1, g_row, 0.0).astype(BF16)
            right = jnp.where(sub == i2, 1.0, 0.0).astype(BF16)
            gs_ref[pl.ds(t, PEER_NKEYS, stride=stride), :] = _dot_nt(left, right)
            return 0

        lax.fori_loop(0, tt, tok, 0, unroll=BUILD_UNROLL)

    hid = _dot_nt(xn_ref[...], u_ref[...])
    slabs = []
    for al in range(ce // LANES):
        start = pl.multiple_of((c * (ce // LANES) + al) * stride, 8)
        slabs.append(gs_ref[pl.ds(start, tt), :])
    gate = jnp.concatenate(slabs, axis=1)
    a = (gate * _gelu(hid)).astype(BF16)
    o_ref[...] += _dot(a, v_ref[...])


def _peer_dense(xn, e, g, u, v, layer):
    n = xn.shape[0]
    tt = PEER_TILE
    ce = PEER_CHUNK
    row = lambda i, c: (i, 0)
    chunk = lambda i, c: (layer, c, 0)
    return pl.pallas_call(
        _peer_dense_kernel,
        grid=(n // tt, PEER_EXPERTS // ce),
        in_specs=[pl.BlockSpec((tt, D_MODEL), row), pl.BlockSpec((tt, LANES), row),
                  pl.BlockSpec((tt, LANES), row), pl.BlockSpec((None, ce, D_MODEL), chunk),
                  pl.BlockSpec((None, ce, D_MODEL), chunk)],
        out_specs=pl.BlockSpec((tt, D_MODEL), row),
        out_shape=jax.ShapeDtypeStruct((n, D_MODEL), F32),
        scratch_shapes=[pltpu.VMEM((PEER_NKEYS * (tt + GS_PAD), LANES), F32)],
        compiler_params=pltpu.CompilerParams(dimension_semantics=("parallel", "arbitrary"),
                                             vmem_limit_bytes=PEER_VMEM_LIMIT),
        name="peer_dense",
    )(xn, e, g, u, v)


def _ple_kernel(h_ref, peer_ref, p_ref, g_ref, wg_ref, bg_ref, wp_ref, gf_ref, *o_refs, n_prompt_tiles):
    h = h_ref[...] + peer_ref[...]
    xn = _rms(h, g_ref[...]).astype(BF16)
    gate = jax.nn.sigmoid(_dot(xn, wg_ref[...]) + bg_ref[...])
    out = h + _dot(p_ref[...].astype(BF16), wp_ref[...]) * gate
    if len(o_refs) == 1:
        o_refs[0][...] = out
        return
    y = _rms(out, gf_ref[...])
    is_prompt = pl.program_id(0) < n_prompt_tiles

    @pl.when(is_prompt)
    def _():
        o_refs[0][...] = y

    @pl.when(jnp.logical_not(is_prompt))
    def _():
        o_refs[1][...] = y


def _ple(h1, peer_out, p, g, wg, bg, wp, gf, layer, n_prompt, final):
    n = h1.shape[0]
    tm = TOK_TILE
    npt = n_prompt // tm
    row = lambda i: (i, 0)
    fixed = lambda i: (0, 0)
    if final:
        out_specs = [pl.BlockSpec((tm, D_MODEL), lambda i: (jnp.minimum(i, npt - 1), 0)),
                     pl.BlockSpec((tm, D_MODEL), lambda i: (jnp.maximum(i - npt, 0), 0))]
        out_shape = [jax.ShapeDtypeStruct((n_prompt, D_MODEL), F32),
                     jax.ShapeDtypeStruct((n - n_prompt, D_MODEL), F32)]
    else:
        out_specs = [pl.BlockSpec((tm, D_MODEL), row)]
        out_shape = [jax.ShapeDtypeStruct((n, D_MODEL), F32)]
    return pl.pallas_call(
        functools.partial(_ple_kernel, n_prompt_tiles=npt),
        grid=(n // tm,),
        in_specs=[pl.BlockSpec((tm, D_MODEL), row), pl.BlockSpec((tm, D_MODEL), row),
                  pl.BlockSpec((None, tm, PLE_DIM), lambda i: (layer, i, 0)),
                  pl.BlockSpec((1, D_MODEL), fixed), pl.BlockSpec((D_MODEL, D_MODEL), fixed),
                  pl.BlockSpec((1, D_MODEL), fixed), pl.BlockSpec((PLE_DIM, D_MODEL), fixed),
                  pl.BlockSpec((1, D_MODEL), fixed)],
        out_specs=out_specs,
        out_shape=out_shape,
        compiler_params=pltpu.CompilerParams(dimension_semantics=("arbitrary",),
                                             vmem_limit_bytes=VMEM_LIMIT),
        name="ple",
    )(h1, peer_out, p, g, wg, bg, wp, gf)


def kernel(x_prompt, x_sample, cache_k, cache_v, p_prompt, p_sample, g_mix, w_in, g_sb_out, w_spatial, b_spatial, g_a_v, g_a_out, w_out, g_ffn, w_peer_q, peer_sub_keys, peer_u, peer_v, g_ple, w_ple_gate, b_ple_gate, w_ple, g_final):
    depth = w_in.shape[0]
    b, t, d = x_prompt.shape
    bs, ts, _ = x_sample.shape
    past = cache_k.shape[2]
    n_p, n_s = b * t, bs * ts
    assert d == D_MODEL and t % SB_BLOCK == 0 and past % SB_BLOCK == 0 and t % TOK_TILE == 0
    assert n_p % TOK_TILE == 0 and n_s % TOK_TILE == 0 and MIX_L % ts == 0 and n_p % ts == 0
    assert (n_p + n_s) % ROUTE_TILE == 0 and (n_p + n_s) % PEER_TILE == 0

    h = jnp.concatenate([x_prompt.reshape(n_p, d), x_sample.reshape(n_s, d)], axis=0)
    row = lambda a: a.reshape(1, -1)
    p_all = jnp.concatenate([p_prompt.reshape(depth, n_p, PLE_DIM), p_sample.reshape(depth, n_s, PLE_DIM)], axis=1)
    u_tab, v_tab = peer_u.astype(BF16), peer_v.astype(BF16)
    cache_kt = jnp.transpose(cache_k, (0, 1, 3, 4, 2)).reshape(depth, bs, SB_W, past)
    cache_vt = jnp.transpose(cache_v, (0, 1, 3, 4, 2)).reshape(depth, bs, SB_W, past)
    kv = None
    vas = None
    for l in range(depth):
        wkv_t = jnp.transpose(w_in[l][:, SB_W:3 * SB_W]).astype(BF16)
        (q, ktb, vtb, ua, va), kv = _inproj(h, row(g_mix[l]), w_in[l].astype(BF16), wkv_t, l, depth, b, t, kv)
        o = _sb_prompt(q, ktb, vtb, b, t)
        o = _sb_sample(q, kv[2], kv[3], cache_kt, cache_vt, o, l, n_p, ts)
        ws, bsp = _spatial_params(w_spatial[l], b_spatial[l], ts)
        h1, vas = _mixer_out(h, ua, va, o, ws, bsp, row(g_a_v[l]), row(g_a_out[l]), row(g_sb_out[l]),
                             w_out[l].astype(BF16), n_p, l, depth, vas)
        wq = jnp.transpose(w_peer_q[l].reshape(d, PEER_HEADS, LANES), (1, 0, 2)).astype(BF16)
        xn, e, gate = _peer_route(h1, row(g_ffn[l]), wq, _peer_key_matrix(peer_sub_keys[l]))
        peer_out = _peer_dense(xn, e, gate, u_tab, v_tab, l)
        res = _ple(h1, peer_out, p_all, row(g_ple[l]), w_ple_gate[l].astype(BF16), row(b_ple_gate[l]),
                   w_ple[l].astype(BF16), row(g_final), l, n_p, final=(l == depth - 1))
        h = res[0]
    y_p, y_s = res
    k_p, v_p, k_s, v_s = kv

    def untranspose(a):
        return jnp.transpose(a.reshape(depth, b, N_HEADS, HEAD_DIM, t), (0, 1, 4, 2, 3))

    return (y_p.reshape(b, t, d), y_s.reshape(bs, ts, d), untranspose(k_p), untranspose(v_p),
            k_s.reshape(depth, bs, ts, N_HEADS, HEAD_DIM), v_s.reshape(depth, bs, ts, N_HEADS, HEAD_DIM),
            vas.reshape(depth, bs, ts, A_W // GROUP, GROUP))
```

```python
import functools

import jax
import jax.numpy as jnp
from jax import lax
from jax.experimental import pallas as pl
from jax.experimental.pallas import tpu as pltpu

F32 = jnp.float32
BF16 = jnp.bfloat16

LANES = 128
SUBLANES = 8
D_MODEL = 1024
N_HEADS = 8
HEAD_DIM = 64
SB_W = N_HEADS * HEAD_DIM
A_W = 512
IN_W = 3 * SB_W + 2 * A_W
GROUP = 64
MIX_L = 128
PEER_HEADS = 8
PEER_NKEYS = 128
PEER_TOPK = 16
PEER_EXPERTS = PEER_NKEYS * PEER_NKEYS
PLE_DIM = 256
RMS_EPS = 1e-6
TOK_TILE = 512
ROUTE_TILE = SUBLANES * LANES
SB_BLOCK = 256
SB_STEP_HEADS = 8
PEER_TILE = 512
PEER_CHUNK = 2048
PEER_VMEM_LIMIT = 58 * 1024 * 1024
BUILD_UNROLL = 128
GS_PAD = 8
VMEM_LIMIT = 48 * 1024 * 1024
INV_SQRT2 = 0.7071067811865476


def _dot(a, b):
    return jnp.dot(a, b, preferred_element_type=F32)


def _dot_nt(a, b):
    return lax.dot_general(a, b, (((1,), (1,)), ((), ())), preferred_element_type=F32)


def _split_bf16(x):
    hi = x.astype(BF16)
    lo = (x - hi.astype(F32)).astype(BF16)
    return hi, lo


def _gelu(x):
    return 0.5 * x * (1.0 + lax.erf(x * INV_SQRT2))


def _rms(x, g):
    ms = jnp.mean(x * x, axis=-1, keepdims=True)
    return x * lax.rsqrt(ms + RMS_EPS) * g


def _group_mean_matrix():
    r = lax.broadcasted_iota(jnp.int32, (LANES, LANES), 0) // GROUP
    c = lax.broadcasted_iota(jnp.int32, (LANES, LANES), 1) // GROUP
    return jnp.where(r == c, 1.0 / GROUP, 0.0).astype(BF16)


def _group_rms(x, g, bd):
    xx = x * x
    hi, lo = _split_bf16(xx)
    parts = []
    for j in range(x.shape[1] // LANES):
        sl = slice(LANES * j, LANES * (j + 1))
        parts.append(_dot(hi[:, sl], bd) + _dot(lo[:, sl], bd))
    ms = jnp.concatenate(parts, axis=1)
    return x * lax.rsqrt(ms + RMS_EPS) * g


def _inproj_kernel(x_ref, g_ref, w_ref, wkv_t_ref, *refs, n_prompt_tiles):
    q_ref, ktb_ref, vtb_ref, ua_ref, va_ref, kp_ref, vp_ref, ks_ref, vs_ref = refs[-9:]
    xn = _rms(x_ref[...], g_ref[...]).astype(BF16)
    q = _dot(xn, w_ref[:, 0:SB_W])
    q_ref[...] = (q * (HEAD_DIM ** -0.5)).astype(BF16)
    ua_ref[...] = _dot(xn, w_ref[:, 3 * SB_W:3 * SB_W + A_W])
    va_ref[...] = _dot(xn, w_ref[:, 3 * SB_W + A_W:IN_W])
    kt = _dot_nt(wkv_t_ref[0:SB_W, :], xn)
    vt = _dot_nt(wkv_t_ref[SB_W:2 * SB_W, :], xn)
    for j in range(ktb_ref.shape[0]):
        ktb_ref[j] = kt[:, j * SB_BLOCK:(j + 1) * SB_BLOCK].astype(BF16)
        vtb_ref[j] = vt[:, j * SB_BLOCK:(j + 1) * SB_BLOCK].astype(BF16)
    is_prompt = pl.program_id(0) < n_prompt_tiles

    @pl.when(is_prompt)
    def _():
        kp_ref[...] = kt
        vp_ref[...] = vt

    @pl.when(jnp.logical_not(is_prompt))
    def _():
        ks_ref[...] = _dot(xn, w_ref[:, SB_W:2 * SB_W])
        vs_ref[...] = _dot(xn, w_ref[:, 2 * SB_W:3 * SB_W])


def _split_rows_specs(layer, tile, width, n_prompt_tiles):
    prompt = pl.BlockSpec((None, tile, width), lambda i: (layer, jnp.minimum(i, n_prompt_tiles - 1), 0))
    sample = pl.BlockSpec((None, tile, width), lambda i: (layer, jnp.maximum(i - n_prompt_tiles, 0), 0))
    return prompt, sample


def _inproj(h, g, w, wkv_t, layer, depth, batch, seq, stacked):
    n = h.shape[0]
    tm = TOK_TILE
    n_prompt = batch * seq
    npt, per_seq = n_prompt // tm, seq // tm
    row = lambda i: (i, 0)
    fixed = lambda i: (0, 0)
    out_f = jax.ShapeDtypeStruct((n, SB_W), F32)
    out_b = jax.ShapeDtypeStruct((n, SB_W), BF16)
    out_t = jax.ShapeDtypeStruct((n // SB_BLOCK, SB_W, SB_BLOCK), BF16)
    stack_p = jax.ShapeDtypeStruct((depth, batch, SB_W, seq), F32)
    stack_s = jax.ShapeDtypeStruct((depth, n - n_prompt, SB_W), F32)
    blk = pl.BlockSpec((tm, SB_W), row)
    blk_t = pl.BlockSpec((tm // SB_BLOCK, SB_W, SB_BLOCK), lambda i: (i, 0, 0))
    _, sample_blk = _split_rows_specs(layer, tm, SB_W, npt)

    def prompt_idx(i):
        t = jnp.minimum(i, npt - 1)
        return (layer, t // per_seq, 0, t % per_seq)

    prompt_blk = pl.BlockSpec((None, None, SB_W, tm), prompt_idx)
    carried = [] if stacked is None else list(stacked)
    outs = pl.pallas_call(
        functools.partial(_inproj_kernel, n_prompt_tiles=npt),
        grid=(n // tm,),
        in_specs=[pl.BlockSpec((tm, D_MODEL), row), pl.BlockSpec((1, D_MODEL), fixed),
                  pl.BlockSpec((D_MODEL, IN_W), fixed), pl.BlockSpec((2 * SB_W, D_MODEL), fixed)]
                 + [pl.BlockSpec(memory_space=pl.ANY)] * len(carried),
        out_specs=[blk, blk_t, blk_t, blk, blk, prompt_blk, prompt_blk, sample_blk, sample_blk],
        out_shape=[out_b, out_t, out_t, out_f, out_f, stack_p, stack_p, stack_s, stack_s],
        input_output_aliases={4 + j: 5 + j for j in range(len(carried))},
        compiler_params=pltpu.CompilerParams(dimension_semantics=("arbitrary",),
                                             vmem_limit_bytes=VMEM_LIMIT),
        name="inproj",
    )(h, g, w, wkv_t, *carried)
    return outs[:5], tuple(outs[5:])


def _strict_lower_ones(n):
    r = lax.broadcasted_iota(jnp.int32, (n, n), 0)
    c = lax.broadcasted_iota(jnp.int32, (n, n), 1)
    return jnp.where(r > c, 1.0, 0.0).astype(BF16)


def _sb_blocks(chains, ustrict, mask):
    zs = [_dot(qh, kt) for qh, kt, _, _ in chains]
    log_betas, log_1ms, splits = [], [], []
    for z in zs:
        sp = jnp.log(1.0 + jnp.exp(-jnp.abs(z)))
        log_beta = jnp.minimum(z, 0.0) - sp
        log_1m = log_beta - z
        if mask is not None:
            log_1m = jnp.where(mask, log_1m, 0.0)
        log_betas.append(log_beta)
        log_1ms.append(log_1m)
        splits.append(_split_bf16(log_1m))
    laters = [_dot(hi, ustrict) + _dot(lo, ustrict) for hi, lo in splits]
    ws, carries = [], []
    for (_, kt, _, carry), log_beta, log_1m, later in zip(chains, log_betas, log_1ms, laters):
        rest = later + jnp.concatenate([carry] * (kt.shape[1] // LANES), axis=1)
        w = jnp.exp(log_beta + rest)
        if mask is not None:
            w = jnp.where(mask, w, 0.0)
        ws.append(w.astype(BF16))
        carries.append(carry + jnp.broadcast_to(later[:, :1] + log_1m[:, :1], carry.shape))
    return [(_dot_nt(w, vth), carry) for w, (_, _, vth, _), carry in zip(ws, chains, carries)]


def _sb_prompt_kernel(q_ref, kt_ref, vt_ref, o_ref, acc_ref, carry_ref):
    tq = q_ref.shape[0]
    tk = tq
    qi = pl.program_id(2)
    ustrict = _strict_lower_ones(tk)
    r = lax.broadcasted_iota(jnp.int32, (tq, tk), 0)
    c = lax.broadcasted_iota(jnp.int32, (tq, tk), 1)
    diag = c < r
    lane_head = lax.broadcasted_iota(jnp.int32, (1, LANES), 1) // HEAD_DIM
    row_head = lax.broadcasted_iota(jnp.int32, (LANES, 1), 0) // HEAD_DIM
    n_pairs = q_ref.shape[1] // LANES
    lanes = [slice(pr * LANES, (pr + 1) * LANES) for pr in range(n_pairs)]
    qhs = [[jnp.where(lane_head == hh, q_ref[:, ln], jnp.zeros((tq, LANES), BF16)) for hh in range(2)]
           for ln in lanes]
    acc_ref[...] = jnp.zeros_like(acc_ref)
    carry_ref[...] = jnp.zeros_like(carry_ref)

    def step(kb, mask):
        chains = []
        for pr, ln in enumerate(lanes):
            kt = kt_ref[kb, ln, :]
            vt = vt_ref[kb, ln, :]
            for hh in range(2):
                vth = jnp.where(row_head == hh, vt, jnp.zeros_like(vt))
                chains.append((qhs[pr][hh], kt, vth, carry_ref[2 * pr + hh]))
        results = _sb_blocks(chains, ustrict, mask)
        for i, (_, carry) in enumerate(results):
            carry_ref[i] = carry
        for pr, ln in enumerate(lanes):
            acc_ref[:, ln] += results[2 * pr][0] + results[2 * pr + 1][0]

    step(qi, diag)

    def body(it, _):
        step(qi - 1 - it, None)
        return 0

    lax.fori_loop(0, qi, body, 0)
    o_ref[...] = acc_ref[...]


def _sb_prompt(q, ktb, vtb, batch, seq):
    tq = SB_BLOCK
    nq = seq // tq
    width = SB_STEP_HEADS * HEAD_DIM
    return pl.pallas_call(
        _sb_prompt_kernel,
        grid=(batch, SB_W // width, nq),
        in_specs=[pl.BlockSpec((tq, width), lambda b, j, i: (b * nq + i, j)),
                  pl.BlockSpec((nq, width, tq), lambda b, j, i: (b, j, 0)),
                  pl.BlockSpec((nq, width, tq), lambda b, j, i: (b, j, 0))],
        out_specs=pl.BlockSpec((tq, width), lambda b, j, i: (b * nq + i, j)),
        out_shape=jax.ShapeDtypeStruct((q.shape[0], SB_W), F32),
        scratch_shapes=[pltpu.VMEM((tq, width), F32), pltpu.VMEM((SB_STEP_HEADS, tq, LANES), F32)],
        compiler_params=pltpu.CompilerParams(
            dimension_semantics=("parallel", "parallel", "arbitrary"), vmem_limit_bytes=VMEM_LIMIT),
        name="sb_prompt",
    )(q, ktb, vtb)


def _sb_sample_kernel(q_ref, k_ref, v_ref, ck_ref, cv_ref, prompt_rows_ref, o_ref):
    del prompt_rows_ref
    ts = q_ref.shape[0]
    past = ck_ref.shape[1]
    tk = SB_BLOCK
    ustrict = _strict_lower_ones(tk)
    ustrict_d = ustrict[:LANES, :LANES]
    r = lax.broadcasted_iota(jnp.int32, (ts, LANES), 0)
    c = lax.broadcasted_iota(jnp.int32, (ts, LANES), 1)
    diag = c < r
    lane_head = lax.broadcasted_iota(jnp.int32, (1, LANES), 1) // HEAD_DIM
    row_head = lax.broadcasted_iota(jnp.int32, (LANES, 1), 0) // HEAD_DIM
    pad = jnp.zeros((LANES - ts, LANES), F32)
    n_pairs = q_ref.shape[1] // LANES
    lanes = [slice(pr * LANES, (pr + 1) * LANES) for pr in range(n_pairs)]
    heads = [(pr, hh, jnp.where(lane_head == hh, q_ref[:, lanes[pr]], jnp.zeros((ts, LANES), BF16)))
             for pr in range(n_pairs) for hh in range(2)]

    def own(hh, vt):
        return jnp.where(row_head == hh, vt, jnp.zeros_like(vt))

    def new_t(ref, pr):
        return jnp.transpose(jnp.concatenate([ref[:, lanes[pr]], pad], axis=0)).astype(BF16)

    zero = jnp.zeros((ts, LANES), F32)
    new_k = [new_t(k_ref, pr) for pr in range(n_pairs)]
    new_v = [new_t(v_ref, pr) for pr in range(n_pairs)]
    results = _sb_blocks([(qh, new_k[pr], own(hh, new_v[pr]), zero) for pr, hh, qh in heads], ustrict_d, diag)
    accs = [out for out, _ in results]
    for kb in range(past // tk - 1, -1, -1):
        keys = slice(kb * tk, (kb + 1) * tk)
        old_k = [ck_ref[lanes[pr], keys].astype(BF16) for pr in range(n_pairs)]
        old_v = [cv_ref[lanes[pr], keys].astype(BF16) for pr in range(n_pairs)]
        results = _sb_blocks([(qh, old_k[pr], own(hh, old_v[pr]), carry)
                              for (pr, hh, qh), (_, carry) in zip(heads, results)], ustrict, None)
        accs = [acc + out for acc, (out, _) in zip(accs, results)]
    for pr in range(n_pairs):
        o_ref[:, lanes[pr]] = accs[2 * pr] + accs[2 * pr + 1]


def _sb_sample(q, k_s, v_s, cache_kt, cache_vt, o, layer, n_prompt, ts):
    _, batch, width, past = cache_kt.shape
    off = n_prompt // ts
    rows = pl.BlockSpec((ts, SB_W), lambda b: (off + b, 0))
    new = pl.BlockSpec((None, ts, SB_W), lambda b: (layer, b, 0))
    old = pl.BlockSpec((None, None, width, past), lambda b: (layer, b, 0, 0))
    return pl.pallas_call(
        _sb_sample_kernel,
        grid=(batch,),
        in_specs=[rows, new, new, old, old, pl.BlockSpec(memory_space=pl.ANY)],
        out_specs=rows,
        out_shape=jax.ShapeDtypeStruct(o.shape, o.dtype),
        input_output_aliases={5: 0},
        compiler_params=pltpu.CompilerParams(dimension_semantics=("parallel",),
                                             vmem_limit_bytes=VMEM_LIMIT),
        name="sb_sample",
    )(q, k_s, v_s, cache_kt, cache_vt, o)


def _mixer_out_kernel(h_ref, ua_ref, va_ref, o_ref, ws_ref, bs_ref, gav_ref, gao_ref, gsb_ref, wo_ref,
                      *refs, n_prompt_tiles):
    h1_ref, vas_ref = refs[-2:]
    tm = h_ref.shape[0]
    bd = _group_mean_matrix()
    lane_group = lax.broadcasted_iota(jnp.int32, (1, LANES), 1) // GROUP
    u = _gelu(ua_ref[...])
    vn = _group_rms(_gelu(va_ref[...]), gav_ref[...], bd)

    @pl.when(pl.program_id(0) >= n_prompt_tiles)
    def _():
        vas_ref[...] = vn

    vnb = vn.astype(BF16)
    rows = []
    for rr in range(tm // MIX_L):
        slabs = []
        for j in range(A_W // LANES):
            v2 = vnb[rr * MIX_L:(rr + 1) * MIX_L, j * LANES:(j + 1) * LANES]
            mix = jnp.zeros((MIX_L, LANES), F32)
            for gg in range(2):
                v2g = jnp.where(lane_group == gg, v2, jnp.zeros_like(v2))
                mix = mix + _dot(ws_ref[2 * j + gg], v2g)
            slabs.append(mix)
        rows.append(jnp.concatenate(slabs, axis=1) + bs_ref[...])
    mix = jnp.concatenate(rows, axis=0)
    oa = _group_rms(u * mix, gao_ref[...], bd)
    osb = _group_rms(o_ref[...], gsb_ref[...], bd)
    cat = jnp.concatenate([osb, oa], axis=1).astype(BF16)
    h1_ref[...] = h_ref[...] + _dot(cat, wo_ref[...])


def _mixer_out(h, ua, va, o, ws, bs, gav, gao, gsb, wo, n_prompt, layer, depth, stacked):
    n = h.shape[0]
    tm = TOK_TILE
    npt = n_prompt // tm
    row = lambda i: (i, 0)
    fixed = lambda i: (0, 0)
    variant = lambda i: jnp.where(i < npt, 0, 1)
    _, sample_blk = _split_rows_specs(layer, tm, A_W, npt)
    carried = [] if stacked is None else [stacked]
    return pl.pallas_call(
        functools.partial(_mixer_out_kernel, n_prompt_tiles=npt),
        grid=(n // tm,),
        in_specs=[pl.BlockSpec((tm, D_MODEL), row), pl.BlockSpec((tm, A_W), row),
                  pl.BlockSpec((tm, A_W), row), pl.BlockSpec((tm, SB_W), row),
                  pl.BlockSpec((None, 2 * A_W // LANES, MIX_L, MIX_L), lambda i: (variant(i), 0, 0, 0)),
                  pl.BlockSpec((None, MIX_L, A_W), lambda i: (variant(i), 0, 0)),
                  pl.BlockSpec((1, A_W), fixed), pl.BlockSpec((1, A_W), fixed),
                  pl.BlockSpec((1, SB_W), fixed), pl.BlockSpec((D_MODEL, D_MODEL), fixed)]
                 + [pl.BlockSpec(memory_space=pl.ANY)] * len(carried),
        out_specs=[pl.BlockSpec((tm, D_MODEL), row), sample_blk],
        out_shape=[jax.ShapeDtypeStruct((n, D_MODEL), F32),
                   jax.ShapeDtypeStruct((depth, n - n_prompt, A_W), F32)],
        input_output_aliases={10: 1} if carried else {},
        compiler_params=pltpu.CompilerParams(dimension_semantics=("arbitrary",),
                                             vmem_limit_bytes=VMEM_LIMIT),
        name="mixer_out",
    )(h, ua, va, o, ws, bs, gav, gao, gsb, wo, *carried)


def _spatial_params(w_s, b_s, ts):
    tril = jnp.tril(jnp.ones((MIX_L, MIX_L), bool))
    w_p = jnp.where(tril[None], w_s, 0.0)
    rep = MIX_L // ts
    w_small = jnp.where(tril[None, :ts, :ts], w_s[:, :ts, :ts], 0.0)
    eye = jnp.eye(rep, dtype=w_s.dtype)
    w_smp = jnp.einsum("ab,gts->gatbs", eye, w_small).reshape(-1, MIX_L, MIX_L)
    b_p = jnp.repeat(jnp.transpose(b_s), A_W // b_s.shape[0], axis=1)
    b_smp = jnp.tile(b_p[:ts], (rep, 1))
    return jnp.stack([w_p, w_smp]).astype(BF16), jnp.stack([b_p, b_smp])


def _argmax_tree(leaf, lo, hi):
    if hi - lo == 1:
        return leaf(lo)
    mid = (lo + hi) // 2
    va, ta = _argmax_tree(leaf, lo, mid)
    vb, tb = _argmax_tree(leaf, mid, hi)
    return jnp.maximum(va, vb), jnp.where(va >= vb, ta, tb)


def _vreg_rows(idx):
    if isinstance(idx, int):
        return slice(idx * SUBLANES, (idx + 1) * SUBLANES)
    return pl.ds(pl.multiple_of(idx * SUBLANES, SUBLANES), SUBLANES)


_PEER_PAIRS = [(i, j) for i in range(PEER_TOPK) for j in range(PEER_TOPK) if (i + 1) * (j + 1) <= PEER_TOPK]


def _peer_route_kernel(h_ref, g_ref, wq_ref, km_ref, xn_ref, e_ref, gate_ref,
                       s_ref, topv_ref, topi_ref, cv_ref, ce_ref, sc_ref, rese_ref, resg_ref):
    k_top = PEER_TOPK
    xn = _rms(h_ref[...], g_ref[...]).astype(BF16)
    xn_ref[...] = xn
    neg_inf = -jnp.inf
    none_yet = jnp.full((SUBLANES, LANES), -1, jnp.int32)

    def head(hd, _):
        pq = _dot(xn, wq_ref[hd]).astype(BF16)
        for c in range(SUBLANES):
            st = _dot_nt(km_ref[hd], pq[c * LANES:(c + 1) * LANES])
            s_ref[pl.ds(c, 2 * PEER_NKEYS, stride=SUBLANES), :] = st

        for p in range(2):
            def extract(k, prev, p=p):
                def leaf(n):
                    r = _vreg_rows(p * PEER_NKEYS + n)
                    v = jnp.where(prev == n, neg_inf, s_ref[r, :])
                    s_ref[r, :] = v
                    return v, n
                m, im = _argmax_tree(leaf, 0, PEER_NKEYS)
                topv_ref[_vreg_rows(p * k_top + k), :] = m
                topi_ref[_vreg_rows(p * k_top + k), :] = im
                return im
            lax.fori_loop(0, k_top, extract, none_yet)

        for idx, (i, j) in enumerate(_PEER_PAIRS):
            cv_ref[_vreg_rows(idx), :] = topv_ref[_vreg_rows(i), :] + topv_ref[_vreg_rows(k_top + j), :]
            ce_ref[_vreg_rows(idx), :] = (topi_ref[_vreg_rows(i), :] * PEER_NKEYS
                                          + topi_ref[_vreg_rows(k_top + j), :])

        def pick(k, prev):
            def leaf(idx):
                r = _vreg_rows(idx)
                eid = ce_ref[r, :]
                v = jnp.where(eid == prev, neg_inf, cv_ref[r, :])
                cv_ref[r, :] = v
                return v, eid
            m, em = _argmax_tree(leaf, 0, len(_PEER_PAIRS))
            sc_ref[_vreg_rows(k), :] = m
            rese_ref[_vreg_rows(hd * k_top + k), :] = em.astype(F32)
            return em
        lax.fori_loop(0, k_top, pick, none_yet)

        mx = sc_ref[_vreg_rows(0), :]
        exs = [jnp.exp(sc_ref[_vreg_rows(k), :] - mx) for k in range(k_top)]
        total = exs[0]
        for ex in exs[1:]:
            total = total + ex
        for k in range(k_top):
            resg_ref[_vreg_rows(hd * k_top + k), :] = exs[k] / total
        return 0

    lax.fori_loop(0, PEER_HEADS, head, 0)
    for c in range(SUBLANES):
        rows = slice(c * LANES, (c + 1) * LANES)
        e_ref[rows, :] = jnp.transpose(rese_ref[pl.ds(c, LANES, stride=SUBLANES), :]).astype(jnp.int32)
        gate_ref[rows, :] = jnp.transpose(resg_ref[pl.ds(c, LANES, stride=SUBLANES), :])


def _peer_route(h1, g, wq, km):
    n = h1.shape[0]
    tm = ROUTE_TILE
    row = lambda i: (i, 0)
    vregs = lambda count, dtype: pltpu.VMEM((count * SUBLANES, LANES), dtype)
    return pl.pallas_call(
        _peer_route_kernel,
        grid=(n // tm,),
        in_specs=[pl.BlockSpec((tm, D_MODEL), row), pl.BlockSpec((1, D_MODEL), lambda i: (0, 0)),
                  pl.BlockSpec((PEER_HEADS, D_MODEL, LANES), lambda i: (0, 0, 0)),
                  pl.BlockSpec((PEER_HEADS, 2 * PEER_NKEYS, LANES), lambda i: (0, 0, 0))],
        out_specs=[pl.BlockSpec((tm, D_MODEL), row), pl.BlockSpec((tm, LANES), row),
                   pl.BlockSpec((tm, LANES), row)],
        out_shape=[jax.ShapeDtypeStruct((n, D_MODEL), BF16),
                   jax.ShapeDtypeStruct((n, LANES), jnp.int32),
                   jax.ShapeDtypeStruct((n, LANES), F32)],
        scratch_shapes=[vregs(2 * PEER_NKEYS, F32), vregs(2 * PEER_TOPK, F32), vregs(2 * PEER_TOPK, jnp.int32),
                        vregs(len(_PEER_PAIRS), F32), vregs(len(_PEER_PAIRS), jnp.int32),
                        vregs(PEER_TOPK, F32), vregs(PEER_HEADS * PEER_TOPK, F32),
                        vregs(PEER_HEADS * PEER_TOPK, F32)],
        compiler_params=pltpu.CompilerParams(dimension_semantics=("parallel",),
                                             vmem_limit_bytes=VMEM_LIMIT),
        name="peer_route",
    )(h1, g, wq, km)


def _peer_key_matrix(sub_keys):
    z = jnp.zeros_like(sub_keys[:, 0])
    k0 = jnp.concatenate([sub_keys[:, 0], z], axis=2)
    k1 = jnp.concatenate([z, sub_keys[:, 1]], axis=2)
    return jnp.concatenate([k0, k1], axis=1).astype(BF16)


def _peer_dense_kernel(xn_ref, e_ref, g_ref, u_ref, v_ref, o_ref, gs_ref):
    tt = xn_ref.shape[0]
    ce = u_ref.shape[0]
    half = tt // 2
    stride = half + GS_PAD
    c = pl.program_id(1)

    @pl.when(c == 0)
    def _():
        o_ref[...] = jnp.zeros_like(o_ref)
        sub = lax.broadcasted_iota(jnp.int32, (PEER_NKEYS, LANES), 0)

        def gate_map(t):
            e_row = e_ref[pl.ds(t, 1), :]
            g_row = g_ref[pl.ds(t, 1), :]
            left = jnp.where(sub == e_row // PEER_NKEYS, g_row, 0.0).astype(BF16)
            right = jnp.where(sub == e_row % PEER_NKEYS, 1.0, 0.0).astype(BF16)
            return _dot_nt(left, right)

        def tok(t, _):
            lo = lax.bitcast_convert_type(gate_map(t), jnp.uint32)
            hi = lax.bitcast_convert_type(gate_map(t + half), jnp.uint32)
            gs_ref[pl.ds(t, PEER_NKEYS, stride=stride), :] = (lo >> 16) | (hi & jnp.uint32(0xFFFF0000))
            return 0

        lax.fori_loop(0, half, tok, 0, unroll=BUILD_UNROLL // 2)

    hid = _dot_nt(xn_ref[...], u_ref[...])
    slabs = []
    for al in range(ce // LANES):
        start = pl.multiple_of((c * (ce // LANES) + al) * stride, SUBLANES)
        packed = gs_ref[pl.ds(start, half), :]
        slabs.append(jnp.concatenate([lax.bitcast_convert_type(packed << 16, F32),
                                      lax.bitcast_convert_type(packed & jnp.uint32(0xFFFF0000), F32)], axis=0))
    gate = jnp.concatenate(slabs, axis=1)
    a = (gate * _gelu(hid)).astype(BF16)
    o_ref[...] += _dot(a, v_ref[...])


def _peer_dense(xn, e, g, u, v, layer):
    n = xn.shape[0]
    tt = PEER_TILE
    ce = PEER_CHUNK
    row = lambda i, c: (i, 0)
    chunk = lambda i, c: (layer, c, 0)
    return pl.pallas_call(
        _peer_dense_kernel,
        grid=(n // tt, PEER_EXPERTS // ce),
        in_specs=[pl.BlockSpec((tt, D_MODEL), row), pl.BlockSpec((tt, LANES), row),
                  pl.BlockSpec((tt, LANES), row), pl.BlockSpec((None, ce, D_MODEL), chunk),
                  pl.BlockSpec((None, ce, D_MODEL), chunk)],
        out_specs=pl.BlockSpec((tt, D_MODEL), row),
        out_shape=jax.ShapeDtypeStruct((n, D_MODEL), F32),
        scratch_shapes=[pltpu.VMEM((PEER_NKEYS * (tt // 2 + GS_PAD), LANES), jnp.uint32)],
        compiler_params=pltpu.CompilerParams(dimension_semantics=("parallel", "arbitrary"),
                                             vmem_limit_bytes=PEER_VMEM_LIMIT),
        name="peer_dense",
    )(xn, e, g, u, v)


def _ple_kernel(h_ref, peer_ref, p_ref, g_ref, wg_ref, bg_ref, wp_ref, gf_ref, *o_refs, n_prompt_tiles):
    h = h_ref[...] + peer_ref[...]
    xn = _rms(h, g_ref[...]).astype(BF16)
    gate = jax.nn.sigmoid(_dot(xn, wg_ref[...]) + bg_ref[...])
    out = h + _dot(p_ref[...].astype(BF16), wp_ref[...]) * gate
    if len(o_refs) == 1:
        o_refs[0][...] = out
        return
    y = _rms(out, gf_ref[...])
    is_prompt = pl.program_id(0) < n_prompt_tiles

    @pl.when(is_prompt)
    def _():
        o_refs[0][...] = y

    @pl.when(jnp.logical_not(is_prompt))
    def _():
        o_refs[1][...] = y


def _ple(h1, peer_out, p, g, wg, bg, wp, gf, layer, n_prompt, final):
    n = h1.shape[0]
    tm = TOK_TILE
    npt = n_prompt // tm
    row = lambda i: (i, 0)
    fixed = lambda i: (0, 0)
    if final:
        out_specs = [pl.BlockSpec((tm, D_MODEL), lambda i: (jnp.minimum(i, npt - 1), 0)),
                     pl.BlockSpec((tm, D_MODEL), lambda i: (jnp.maximum(i - npt, 0), 0))]
        out_shape = [jax.ShapeDtypeStruct((n_prompt, D_MODEL), F32),
                     jax.ShapeDtypeStruct((n - n_prompt, D_MODEL), F32)]
    else:
        out_specs = [pl.BlockSpec((tm, D_MODEL), row)]
        out_shape = [jax.ShapeDtypeStruct((n, D_MODEL), F32)]
    return pl.pallas_call(
        functools.partial(_ple_kernel, n_prompt_tiles=npt),
        grid=(n // tm,),
        in_specs=[pl.BlockSpec((tm, D_MODEL), row), pl.BlockSpec((tm, D_MODEL), row),
                  pl.BlockSpec((None, tm, PLE_DIM), lambda i: (layer, i, 0)),
                  pl.BlockSpec((1, D_MODEL), fixed), pl.BlockSpec((D_MODEL, D_MODEL), fixed),
                  pl.BlockSpec((1, D_MODEL), fixed), pl.BlockSpec((PLE_DIM, D_MODEL), fixed),
                  pl.BlockSpec((1, D_MODEL), fixed)],
        out_specs=out_specs,
        out_shape=out_shape,
        compiler_params=pltpu.CompilerParams(dimension_semantics=("arbitrary",),
                                             vmem_limit_bytes=VMEM_LIMIT),
        name="ple",
    )(h1, peer_out, p, g, wg, bg, wp, gf)


def kernel(x_prompt, x_sample, cache_k, cache_v, p_prompt, p_sample, g_mix, w_in, g_sb_out, w_spatial, b_spatial, g_a_v, g_a_out, w_out, g_ffn, w_peer_q, peer_sub_keys, peer_u, peer_v, g_ple, w_ple_gate, b_ple_gate, w_ple, g_final):
    depth = w_in.shape[0]
    b, t, d = x_prompt.shape
    bs, ts, _ = x_sample.shape
    past = cache_k.shape[2]
    n_p, n_s = b * t, bs * ts
    assert d == D_MODEL and t % SB_BLOCK == 0 and past % SB_BLOCK == 0 and t % TOK_TILE == 0
    assert n_p % TOK_TILE == 0 and n_s % TOK_TILE == 0 and MIX_L % ts == 0 and n_p % ts == 0
    assert (n_p + n_s) % ROUTE_TILE == 0 and (n_p + n_s) % PEER_TILE == 0

    h = jnp.concatenate([x_prompt.reshape(n_p, d), x_sample.reshape(n_s, d)], axis=0)
    row = lambda a: a.reshape(1, -1)
    p_all = jnp.concatenate([p_prompt.reshape(depth, n_p, PLE_DIM), p_sample.reshape(depth, n_s, PLE_DIM)], axis=1)
    u_tab, v_tab = peer_u.astype(BF16), peer_v.astype(BF16)
    cache_kt = jnp.transpose(cache_k, (0, 1, 3, 4, 2)).reshape(depth, bs, SB_W, past)
    cache_vt = jnp.transpose(cache_v, (0, 1, 3, 4, 2)).reshape(depth, bs, SB_W, past)
    kv = None
    vas = None
    for l in range(depth):
        wkv_t = jnp.transpose(w_in[l][:, SB_W:3 * SB_W]).astype(BF16)
        (q, ktb, vtb, ua, va), kv = _inproj(h, row(g_mix[l]), w_in[l].astype(BF16), wkv_t, l, depth, b, t, kv)
        o = _sb_prompt(q, ktb, vtb, b, t)
        o = _sb_sample(q, kv[2], kv[3], cache_kt, cache_vt, o, l, n_p, ts)
        ws, bsp = _spatial_params(w_spatial[l], b_spatial[l], ts)
        h1, vas = _mixer_out(h, ua, va, o, ws, bsp, row(g_a_v[l]), row(g_a_out[l]), row(g_sb_out[l]),
                             w_out[l].astype(BF16), n_p, l, depth, vas)
        wq = jnp.transpose(w_peer_q[l].reshape(d, PEER_HEADS, LANES), (1, 0, 2)).astype(BF16)
        xn, e, gate = _peer_route(h1, row(g_ffn[l]), wq, _peer_key_matrix(peer_sub_keys[l]))
        peer_out = _peer_dense(xn, e, gate, u_tab, v_tab, l)
        res = _ple(h1, peer_out, p_all, row(g_ple[l]), w_ple_gate[l].astype(BF16), row(b_ple_gate[l]),
                   w_ple[l].astype(BF16), row(g_final), l, n_p, final=(l == depth - 1))
        h = res[0]
    y_p, y_s = res
    k_p, v_p, k_s, v_s = kv

    def untranspose(a):
        return jnp.transpose(a.reshape(depth, b, N_HEADS, HEAD_DIM, t), (0, 1, 4, 2, 3))

    return (y_p.reshape(b, t, d), y_s.reshape(bs, ts, d), untranspose(k_p), untranspose(v_p),
            k_s.reshape(depth, bs, ts, N_HEADS, HEAD_DIM), v_s.reshape(depth, bs, ts, N_HEADS, HEAD_DIM),
            vas.reshape(depth, bs, ts, A_W // GROUP, GROUP))
```

```python
import functools

import jax
import jax.numpy as jnp
from jax import lax
from jax.experimental import pallas as pl
from jax.experimental.pallas import tpu as pltpu

F32 = jnp.float32
BF16 = jnp.bfloat16

LANES = 128
SUBLANES = 8
D_MODEL = 1024
N_HEADS = 8
HEAD_DIM = 64
SB_W = N_HEADS * HEAD_DIM
A_W = 512
IN_W = 3 * SB_W + 2 * A_W
GROUP = 64
MIX_L = 128
PEER_HEADS = 8
PEER_NKEYS = 128
PEER_TOPK = 16
PEER_EXPERTS = PEER_NKEYS * PEER_NKEYS
PLE_DIM = 256
RMS_EPS = 1e-6
TOK_TILE = 512
ROUTE_TILE = SUBLANES * LANES
SB_BLOCK = 256
SB_STEP_HEADS = 8
PEER_TILE = 512
PEER_CHUNK = 2048
PEER_VMEM_LIMIT = 58 * 1024 * 1024
BUILD_UNROLL = 128
GS_PAD = 8
VMEM_LIMIT = 48 * 1024 * 1024
INV_SQRT2 = 0.7071067811865476


def _dot(a, b):
    return jnp.dot(a, b, preferred_element_type=F32)


def _dot_nt(a, b):
    return lax.dot_general(a, b, (((1,), (1,)), ((), ())), preferred_element_type=F32)


def _split_bf16(x):
    hi = x.astype(BF16)
    lo = (x - hi.astype(F32)).astype(BF16)
    return hi, lo


def _gelu(x):
    return 0.5 * x * (1.0 + lax.erf(x * INV_SQRT2))


def _rms(x, g):
    ms = jnp.mean(x * x, axis=-1, keepdims=True)
    return x * lax.rsqrt(ms + RMS_EPS) * g


def _group_mean_matrix():
    r = lax.broadcasted_iota(jnp.int32, (LANES, LANES), 0) // GROUP
    c = lax.broadcasted_iota(jnp.int32, (LANES, LANES), 1) // GROUP
    return jnp.where(r == c, 1.0 / GROUP, 0.0).astype(BF16)


def _group_rms(x, g, bd):
    xx = x * x
    hi, lo = _split_bf16(xx)
    parts = []
    for j in range(x.shape[1] // LANES):
        sl = slice(LANES * j, LANES * (j + 1))
        parts.append(_dot(hi[:, sl], bd) + _dot(lo[:, sl], bd))
    ms = jnp.concatenate(parts, axis=1)
    return x * lax.rsqrt(ms + RMS_EPS) * g


def _inproj_kernel(x_ref, g_ref, w_ref, wkv_t_ref, *refs, n_prompt_tiles):
    q_ref, ktb_ref, vtb_ref, ua_ref, va_ref, kp_ref, vp_ref, ks_ref, vs_ref = refs[-9:]
    xn = _rms(x_ref[...], g_ref[...]).astype(BF16)
    q = _dot(xn, w_ref[:, 0:SB_W])
    q_ref[...] = (q * (HEAD_DIM ** -0.5)).astype(BF16)
    ua_ref[...] = _dot(xn, w_ref[:, 3 * SB_W:3 * SB_W + A_W])
    va_ref[...] = _dot(xn, w_ref[:, 3 * SB_W + A_W:IN_W])
    kvt = _dot_nt(wkv_t_ref[...], xn)
    kt, vt = kvt[0:SB_W], kvt[SB_W:2 * SB_W]
    for j in range(ktb_ref.shape[0]):
        ktb_ref[j] = kt[:, j * SB_BLOCK:(j + 1) * SB_BLOCK].astype(BF16)
        vtb_ref[j] = vt[:, j * SB_BLOCK:(j + 1) * SB_BLOCK].astype(BF16)
    is_prompt = pl.program_id(0) < n_prompt_tiles

    @pl.when(is_prompt)
    def _():
        kp_ref[...] = kt
        vp_ref[...] = vt

    @pl.when(jnp.logical_not(is_prompt))
    def _():
        ks_ref[...] = _dot(xn, w_ref[:, SB_W:2 * SB_W])
        vs_ref[...] = _dot(xn, w_ref[:, 2 * SB_W:3 * SB_W])


def _split_rows_specs(layer, tile, width, n_prompt_tiles):
    prompt = pl.BlockSpec((None, tile, width), lambda i: (layer, jnp.minimum(i, n_prompt_tiles - 1), 0))
    sample = pl.BlockSpec((None, tile, width), lambda i: (layer, jnp.maximum(i - n_prompt_tiles, 0), 0))
    return prompt, sample


def _inproj(h, g, w, wkv_t, layer, depth, batch, seq, stacked):
    n = h.shape[0]
    tm = TOK_TILE
    n_prompt = batch * seq
    npt, per_seq = n_prompt // tm, seq // tm
    row = lambda i: (i, 0)
    fixed = lambda i: (0, 0)
    out_f = jax.ShapeDtypeStruct((n, SB_W), F32)
    out_b = jax.ShapeDtypeStruct((n, SB_W), BF16)
    out_t = jax.ShapeDtypeStruct((n // SB_BLOCK, SB_W, SB_BLOCK), BF16)
    stack_p = jax.ShapeDtypeStruct((depth, batch, SB_W, seq), F32)
    stack_s = jax.ShapeDtypeStruct((depth, n - n_prompt, SB_W), F32)
    blk = pl.BlockSpec((tm, SB_W), row)
    blk_t = pl.BlockSpec((tm // SB_BLOCK, SB_W, SB_BLOCK), lambda i: (i, 0, 0))
    _, sample_blk = _split_rows_specs(layer, tm, SB_W, npt)

    def prompt_idx(i):
        t = jnp.minimum(i, npt - 1)
        return (layer, t // per_seq, 0, t % per_seq)

    prompt_blk = pl.BlockSpec((None, None, SB_W, tm), prompt_idx)
    carried = [] if stacked is None else list(stacked)
    outs = pl.pallas_call(
        functools.partial(_inproj_kernel, n_prompt_tiles=npt),
        grid=(n // tm,),
        in_specs=[pl.BlockSpec((tm, D_MODEL), row), pl.BlockSpec((1, D_MODEL), fixed),
                  pl.BlockSpec((D_MODEL, IN_W), fixed), pl.BlockSpec((2 * SB_W, D_MODEL), fixed)]
                 + [pl.BlockSpec(memory_space=pl.ANY)] * len(carried),
        out_specs=[blk, blk_t, blk_t, blk, blk, prompt_blk, prompt_blk, sample_blk, sample_blk],
        out_shape=[out_b, out_t, out_t, out_f, out_f, stack_p, stack_p, stack_s, stack_s],
        input_output_aliases={4 + j: 5 + j for j in range(len(carried))},
        compiler_params=pltpu.CompilerParams(dimension_semantics=("arbitrary",),
                                             vmem_limit_bytes=VMEM_LIMIT),
        name="inproj",
    )(h, g, w, wkv_t, *carried)
    return outs[:5], tuple(outs[5:])


def _strict_lower_ones(n):
    r = lax.broadcasted_iota(jnp.int32, (n, n), 0)
    c = lax.broadcasted_iota(jnp.int32, (n, n), 1)
    return jnp.where(r > c, 1.0, 0.0).astype(BF16)


def _sb_blocks(chains, ustrict, mask):
    zs = [_dot(qh, kt) for qh, kt, _, _ in chains]
    log_betas, log_1ms, splits = [], [], []
    for z in zs:
        sp = jnp.log(1.0 + jnp.exp(-jnp.abs(z)))
        log_beta = jnp.minimum(z, 0.0) - sp
        log_1m = log_beta - z
        if mask is not None:
            log_1m = jnp.where(mask, log_1m, 0.0)
        log_betas.append(log_beta)
        log_1ms.append(log_1m)
        splits.append(_split_bf16(log_1m))
    laters = [_dot(hi, ustrict) + _dot(lo, ustrict) for hi, lo in splits]
    ws, carries = [], []
    for (_, kt, _, carry), log_beta, log_1m, later in zip(chains, log_betas, log_1ms, laters):
        rest = later + jnp.concatenate([carry] * (kt.shape[1] // LANES), axis=1)
        w = jnp.exp(log_beta + rest)
        if mask is not None:
            w = jnp.where(mask, w, 0.0)
        ws.append(w.astype(BF16))
        carries.append(carry + jnp.broadcast_to(later[:, :1] + log_1m[:, :1], carry.shape))
    return [(_dot_nt(w, vth), carry) for w, (_, _, vth, _), carry in zip(ws, chains, carries)]


def _sb_prompt_kernel(q_ref, kt_ref, vt_ref, o_ref, acc_ref, carry_ref):
    tq = q_ref.shape[0]
    tk = tq
    qi = pl.program_id(2)
    ustrict = _strict_lower_ones(tk)
    r = lax.broadcasted_iota(jnp.int32, (tq, tk), 0)
    c = lax.broadcasted_iota(jnp.int32, (tq, tk), 1)
    diag = c < r
    lane_head = lax.broadcasted_iota(jnp.int32, (1, LANES), 1) // HEAD_DIM
    row_head = lax.broadcasted_iota(jnp.int32, (LANES, 1), 0) // HEAD_DIM
    n_pairs = q_ref.shape[1] // LANES
    lanes = [slice(pr * LANES, (pr + 1) * LANES) for pr in range(n_pairs)]
    qhs = [[jnp.where(lane_head == hh, q_ref[:, ln], jnp.zeros((tq, LANES), BF16)) for hh in range(2)]
           for ln in lanes]
    acc_ref[...] = jnp.zeros_like(acc_ref)
    carry_ref[...] = jnp.zeros_like(carry_ref)

    def step(kb, mask):
        chains = []
        for pr, ln in enumerate(lanes):
            kt = kt_ref[kb, ln, :]
            vt = vt_ref[kb, ln, :]
            for hh in range(2):
                vth = jnp.where(row_head == hh, vt, jnp.zeros_like(vt))
                chains.append((qhs[pr][hh], kt, vth, carry_ref[2 * pr + hh]))
        results = _sb_blocks(chains, ustrict, mask)
        for i, (_, carry) in enumerate(results):
            carry_ref[i] = carry
        for pr, ln in enumerate(lanes):
            acc_ref[:, ln] += results[2 * pr][0] + results[2 * pr + 1][0]

    step(qi, diag)

    def body(it, _):
        step(qi - 1 - it, None)
        return 0

    lax.fori_loop(0, qi, body, 0)
    o_ref[...] = acc_ref[...]


def _sb_prompt(q, ktb, vtb, batch, seq):
    tq = SB_BLOCK
    nq = seq // tq
    width = SB_STEP_HEADS * HEAD_DIM
    return pl.pallas_call(
        _sb_prompt_kernel,
        grid=(batch, SB_W // width, nq),
        in_specs=[pl.BlockSpec((tq, width), lambda b, j, i: (b * nq + i, j)),
                  pl.BlockSpec((nq, width, tq), lambda b, j, i: (b, j, 0)),
                  pl.BlockSpec((nq, width, tq), lambda b, j, i: (b, j, 0))],
        out_specs=pl.BlockSpec((tq, width), lambda b, j, i: (b * nq + i, j)),
        out_shape=jax.ShapeDtypeStruct((q.shape[0], SB_W), F32),
        scratch_shapes=[pltpu.VMEM((tq, width), F32), pltpu.VMEM((SB_STEP_HEADS, tq, LANES), F32)],
        compiler_params=pltpu.CompilerParams(
            dimension_semantics=("parallel", "parallel", "arbitrary"), vmem_limit_bytes=VMEM_LIMIT),
        name="sb_prompt",
    )(q, ktb, vtb)


def _sb_sample_kernel(q_ref, k_ref, v_ref, ck_ref, cv_ref, prompt_rows_ref, o_ref):
    del prompt_rows_ref
    ts = q_ref.shape[0]
    past = ck_ref.shape[1]
    tk = SB_BLOCK
    ustrict = _strict_lower_ones(tk)
    ustrict_d = ustrict[:LANES, :LANES]
    r = lax.broadcasted_iota(jnp.int32, (ts, LANES), 0)
    c = lax.broadcasted_iota(jnp.int32, (ts, LANES), 1)
    diag = c < r
    lane_head = lax.broadcasted_iota(jnp.int32, (1, LANES), 1) // HEAD_DIM
    row_head = lax.broadcasted_iota(jnp.int32, (LANES, 1), 0) // HEAD_DIM
    pad = jnp.zeros((LANES - ts, LANES), F32)
    n_pairs = q_ref.shape[1] // LANES
    lanes = [slice(pr * LANES, (pr + 1) * LANES) for pr in range(n_pairs)]
    heads = [(pr, hh, jnp.where(lane_head == hh, q_ref[:, lanes[pr]], jnp.zeros((ts, LANES), BF16)))
             for pr in range(n_pairs) for hh in range(2)]

    def own(hh, vt):
        return jnp.where(row_head == hh, vt, jnp.zeros_like(vt))

    def new_t(ref, pr):
        return jnp.transpose(jnp.concatenate([ref[:, lanes[pr]], pad], axis=0)).astype(BF16)

    zero = jnp.zeros((ts, LANES), F32)
    new_k = [new_t(k_ref, pr) for pr in range(n_pairs)]
    new_v = [new_t(v_ref, pr) for pr in range(n_pairs)]
    results = _sb_blocks([(qh, new_k[pr], own(hh, new_v[pr]), zero) for pr, hh, qh in heads], ustrict_d, diag)
    accs = [out for out, _ in results]
    for kb in range(past // tk - 1, -1, -1):
        keys = slice(kb * tk, (kb + 1) * tk)
        old_k = [ck_ref[lanes[pr], keys].astype(BF16) for pr in range(n_pairs)]
        old_v = [cv_ref[lanes[pr], keys].astype(BF16) for pr in range(n_pairs)]
        results = _sb_blocks([(qh, old_k[pr], own(hh, old_v[pr]), carry)
                              for (pr, hh, qh), (_, carry) in zip(heads, results)], ustrict, None)
        accs = [acc + out for acc, (out, _) in zip(accs, results)]
    for pr in range(n_pairs):
        o_ref[:, lanes[pr]] = accs[2 * pr] + accs[2 * pr + 1]


def _sb_sample(q, k_s, v_s, cache_kt, cache_vt, o, layer, n_prompt, ts):
    _, batch, width, past = cache_kt.shape
    off = n_prompt // ts
    rows = pl.BlockSpec((ts, SB_W), lambda b: (off + b, 0))
    new = pl.BlockSpec((None, ts, SB_W), lambda b: (layer, b, 0))
    old = pl.BlockSpec((None, None, width, past), lambda b: (layer, b, 0, 0))
    return pl.pallas_call(
        _sb_sample_kernel,
        grid=(batch,),
        in_specs=[rows, new, new, old, old, pl.BlockSpec(memory_space=pl.ANY)],
        out_specs=rows,
        out_shape=jax.ShapeDtypeStruct(o.shape, o.dtype),
        input_output_aliases={5: 0},
        compiler_params=pltpu.CompilerParams(dimension_semantics=("parallel",),
                                             vmem_limit_bytes=VMEM_LIMIT),
        name="sb_sample",
    )(q, k_s, v_s, cache_kt, cache_vt, o)


def _mixer_out_kernel(h_ref, ua_ref, va_ref, o_ref, ws_ref, bs_ref, gav_ref, gao_ref, gsb_ref, wo_ref,
                      *refs, n_prompt_tiles):
    h1_ref, vas_ref = refs[-2:]
    tm = h_ref.shape[0]
    bd = _group_mean_matrix()
    lane_group = lax.broadcasted_iota(jnp.int32, (1, LANES), 1) // GROUP
    u = _gelu(ua_ref[...])
    vn = _group_rms(_gelu(va_ref[...]), gav_ref[...], bd)

    @pl.when(pl.program_id(0) >= n_prompt_tiles)
    def _():
        vas_ref[...] = vn

    vnb = vn.astype(BF16)
    rows = []
    for rr in range(tm // MIX_L):
        slabs = []
        for j in range(A_W // LANES):
            v2 = vnb[rr * MIX_L:(rr + 1) * MIX_L, j * LANES:(j + 1) * LANES]
            mix = jnp.zeros((MIX_L, LANES), F32)
            for gg in range(2):
                v2g = jnp.where(lane_group == gg, v2, jnp.zeros_like(v2))
                mix = mix + _dot(ws_ref[2 * j + gg], v2g)
            slabs.append(mix)
        rows.append(jnp.concatenate(slabs, axis=1) + bs_ref[...])
    mix = jnp.concatenate(rows, axis=0)
    oa = _group_rms(u * mix, gao_ref[...], bd)
    osb = _group_rms(o_ref[...], gsb_ref[...], bd)
    cat = jnp.concatenate([osb, oa], axis=1).astype(BF16)
    h1_ref[...] = h_ref[...] + _dot(cat, wo_ref[...])


def _mixer_out(h, ua, va, o, ws, bs, gav, gao, gsb, wo, n_prompt, layer, depth, stacked):
    n = h.shape[0]
    tm = TOK_TILE
    npt = n_prompt // tm
    row = lambda i: (i, 0)
    fixed = lambda i: (0, 0)
    variant = lambda i: jnp.where(i < npt, 0, 1)
    _, sample_blk = _split_rows_specs(layer, tm, A_W, npt)
    carried = [] if stacked is None else [stacked]
    return pl.pallas_call(
        functools.partial(_mixer_out_kernel, n_prompt_tiles=npt),
        grid=(n // tm,),
        in_specs=[pl.BlockSpec((tm, D_MODEL), row), pl.BlockSpec((tm, A_W), row),
                  pl.BlockSpec((tm, A_W), row), pl.BlockSpec((tm, SB_W), row),
                  pl.BlockSpec((None, 2 * A_W // LANES, MIX_L, MIX_L), lambda i: (variant(i), 0, 0, 0)),
                  pl.BlockSpec((None, MIX_L, A_W), lambda i: (variant(i), 0, 0)),
                  pl.BlockSpec((1, A_W), fixed), pl.BlockSpec((1, A_W), fixed),
                  pl.BlockSpec((1, SB_W), fixed), pl.BlockSpec((D_MODEL, D_MODEL), fixed)]
                 + [pl.BlockSpec(memory_space=pl.ANY)] * len(carried),
        out_specs=[pl.BlockSpec((tm, D_MODEL), row), sample_blk],
        out_shape=[jax.ShapeDtypeStruct((n, D_MODEL), F32),
                   jax.ShapeDtypeStruct((depth, n - n_prompt, A_W), F32)],
        input_output_aliases={10: 1} if carried else {},
        compiler_params=pltpu.CompilerParams(dimension_semantics=("arbitrary",),
                                             vmem_limit_bytes=VMEM_LIMIT),
        name="mixer_out",
    )(h, ua, va, o, ws, bs, gav, gao, gsb, wo, *carried)


def _spatial_params(w_s, b_s, ts):
    tril = jnp.tril(jnp.ones((MIX_L, MIX_L), bool))
    w_p = jnp.where(tril[None], w_s, 0.0)
    rep = MIX_L // ts
    w_small = jnp.where(tril[None, :ts, :ts], w_s[:, :ts, :ts], 0.0)
    eye = jnp.eye(rep, dtype=w_s.dtype)
    w_smp = jnp.einsum("ab,gts->gatbs", eye, w_small).reshape(-1, MIX_L, MIX_L)
    b_p = jnp.repeat(jnp.transpose(b_s), A_W // b_s.shape[0], axis=1)
    b_smp = jnp.tile(b_p[:ts], (rep, 1))
    return jnp.stack([w_p, w_smp]).astype(BF16), jnp.stack([b_p, b_smp])


def _argmax_tree(leaf, lo, hi):
    if hi - lo == 1:
        return leaf(lo)
    mid = (lo + hi) // 2
    va, ta = _argmax_tree(leaf, lo, mid)
    vb, tb = _argmax_tree(leaf, mid, hi)
    return jnp.maximum(va, vb), jnp.where(va >= vb, ta, tb)


def _vreg_rows(idx):
    if isinstance(idx, int):
        return slice(idx * SUBLANES, (idx + 1) * SUBLANES)
    return pl.ds(pl.multiple_of(idx * SUBLANES, SUBLANES), SUBLANES)


_PEER_PAIRS = [(i, j) for i in range(PEER_TOPK) for j in range(PEER_TOPK) if (i + 1) * (j + 1) <= PEER_TOPK]


def _peer_route_kernel(h_ref, g_ref, wq_ref, km_ref, xn_ref, e_ref, gate_ref,
                       s_ref, topv_ref, topi_ref, cv_ref, ce_ref, sc_ref, rese_ref, resg_ref):
    k_top = PEER_TOPK
    xn = _rms(h_ref[...], g_ref[...]).astype(BF16)
    xn_ref[...] = xn
    neg_inf = -jnp.inf
    none_yet = jnp.full((SUBLANES, LANES), -1, jnp.int32)

    def head(hd, _):
        pq = _dot(xn, wq_ref[hd]).astype(BF16)
        for c in range(SUBLANES):
            st = _dot_nt(km_ref[hd], pq[c * LANES:(c + 1) * LANES])
            s_ref[pl.ds(c, 2 * PEER_NKEYS, stride=SUBLANES), :] = st

        for p in range(2):
            def extract(k, prev, p=p):
                def leaf(n):
                    r = _vreg_rows(p * PEER_NKEYS + n)
                    v = jnp.where(prev == n, neg_inf, s_ref[r, :])
                    s_ref[r, :] = v
                    return v, n
                m, im = _argmax_tree(leaf, 0, PEER_NKEYS)
                topv_ref[_vreg_rows(p * k_top + k), :] = m
                topi_ref[_vreg_rows(p * k_top + k), :] = im
                return im
            lax.fori_loop(0, k_top, extract, none_yet)

        for idx, (i, j) in enumerate(_PEER_PAIRS):
            cv_ref[_vreg_rows(idx), :] = topv_ref[_vreg_rows(i), :] + topv_ref[_vreg_rows(k_top + j), :]
            ce_ref[_vreg_rows(idx), :] = (topi_ref[_vreg_rows(i), :] * PEER_NKEYS
                                          + topi_ref[_vreg_rows(k_top + j), :])

        def pick(k, prev):
            def leaf(idx):
                r = _vreg_rows(idx)
                eid = ce_ref[r, :]
                v = jnp.where(eid == prev, neg_inf, cv_ref[r, :])
                cv_ref[r, :] = v
                return v, eid
            m, em = _argmax_tree(leaf, 0, len(_PEER_PAIRS))
            sc_ref[_vreg_rows(k), :] = m
            rese_ref[_vreg_rows(hd * k_top + k), :] = em.astype(F32)
            return em
        lax.fori_loop(0, k_top, pick, none_yet)

        mx = sc_ref[_vreg_rows(0), :]
        exs = [jnp.exp(sc_ref[_vreg_rows(k), :] - mx) for k in range(k_top)]
        total = exs[0]
        for ex in exs[1:]:
            total = total + ex
        for k in range(k_top):
            resg_ref[_vreg_rows(hd * k_top + k), :] = exs[k] / total
        return 0

    lax.fori_loop(0, PEER_HEADS, head, 0)
    for c in range(SUBLANES):
        rows = slice(c * LANES, (c + 1) * LANES)
        e_ref[rows, :] = jnp.transpose(rese_ref[pl.ds(c, LANES, stride=SUBLANES), :]).astype(jnp.int32)
        gate_ref[rows, :] = jnp.transpose(resg_ref[pl.ds(c, LANES, stride=SUBLANES), :])


def _peer_route(h1, g, wq, km):
    n = h1.shape[0]
    tm = ROUTE_TILE
    row = lambda i: (i, 0)
    vregs = lambda count, dtype: pltpu.VMEM((count * SUBLANES, LANES), dtype)
    return pl.pallas_call(
        _peer_route_kernel,
        grid=(n // tm,),
        in_specs=[pl.BlockSpec((tm, D_MODEL), row), pl.BlockSpec((1, D_MODEL), lambda i: (0, 0)),
                  pl.BlockSpec((PEER_HEADS, D_MODEL, LANES), lambda i: (0, 0, 0)),
                  pl.BlockSpec((PEER_HEADS, 2 * PEER_NKEYS, LANES), lambda i: (0, 0, 0))],
        out_specs=[pl.BlockSpec((tm, D_MODEL), row), pl.BlockSpec((tm, LANES), row),
                   pl.BlockSpec((tm, LANES), row)],
        out_shape=[jax.ShapeDtypeStruct((n, D_MODEL), BF16),
                   jax.ShapeDtypeStruct((n, LANES), jnp.int32),
                   jax.ShapeDtypeStruct((n, LANES), F32)],
        scratch_shapes=[vregs(2 * PEER_NKEYS, F32), vregs(2 * PEER_TOPK, F32), vregs(2 * PEER_TOPK, jnp.int32),
                        vregs(len(_PEER_PAIRS), F32), vregs(len(_PEER_PAIRS), jnp.int32),
                        vregs(PEER_TOPK, F32), vregs(PEER_HEADS * PEER_TOPK, F32),
                        vregs(PEER_HEADS * PEER_TOPK, F32)],
        compiler_params=pltpu.CompilerParams(dimension_semantics=("parallel",),
                                             vmem_limit_bytes=VMEM_LIMIT),
        name="peer_route",
    )(h1, g, wq, km)


def _peer_key_matrix(sub_keys):
    z = jnp.zeros_like(sub_keys[:, 0])
    k0 = jnp.concatenate([sub_keys[:, 0], z], axis=2)
    k1 = jnp.concatenate([z, sub_keys[:, 1]], axis=2)
    return jnp.concatenate([k0, k1], axis=1).astype(BF16)


def _peer_dense_kernel(xn_ref, e_ref, g_ref, u_ref, v_ref, o_ref, gs_ref):
    tt = xn_ref.shape[0]
    ce = u_ref.shape[0]
    stride = tt + GS_PAD
    c = pl.program_id(1)

    @pl.when(c == 0)
    def _():
        o_ref[...] = jnp.zeros_like(o_ref)
        sub = lax.broadcasted_iota(jnp.int32, (PEER_NKEYS, LANES), 0)

        def tok(t, _):
            e_row = e_ref[pl.ds(t, 1), :]
            g_row = g_ref[pl.ds(t, 1), :]
            left = jnp.where(sub == e_row // PEER_NKEYS, g_row, 0.0).astype(BF16)
            right = jnp.where(sub == e_row % PEER_NKEYS, 1.0, 0.0).astype(BF16)
            gmap = _dot_nt(left, right).astype(BF16)
            gs_ref[pl.ds(t, PEER_NKEYS // 2, stride=stride), :] = pltpu.bitcast(gmap, jnp.uint32)
            return 0

        lax.fori_loop(0, tt, tok, 0, unroll=BUILD_UNROLL)

    hid = _dot_nt(xn_ref[...], u_ref[...])
    slabs = []
    for j in range(ce // LANES // 2):
        start = pl.multiple_of((c * (ce // LANES // 2) + j) * stride, SUBLANES)
        packed = gs_ref[pl.ds(start, tt), :]
        slabs.append(lax.bitcast_convert_type(packed << 16, F32))
        slabs.append(lax.bitcast_convert_type(packed & jnp.uint32(0xFFFF0000), F32))
    gate = jnp.concatenate(slabs, axis=1)
    a = (gate * _gelu(hid)).astype(BF16)
    o_ref[...] += _dot(a, v_ref[...])


def _peer_dense(xn, e, g, u, v, layer):
    n = xn.shape[0]
    tt = PEER_TILE
    ce = PEER_CHUNK
    row = lambda i, c: (i, 0)
    chunk = lambda i, c: (layer, c, 0)
    return pl.pallas_call(
        _peer_dense_kernel,
        grid=(n // tt, PEER_EXPERTS // ce),
        in_specs=[pl.BlockSpec((tt, D_MODEL), row), pl.BlockSpec((tt, LANES), row),
                  pl.BlockSpec((tt, LANES), row), pl.BlockSpec((None, ce, D_MODEL), chunk),
                  pl.BlockSpec((None, ce, D_MODEL), chunk)],
        out_specs=pl.BlockSpec((tt, D_MODEL), row),
        out_shape=jax.ShapeDtypeStruct((n, D_MODEL), F32),
        scratch_shapes=[pltpu.VMEM((PEER_NKEYS // 2 * (tt + GS_PAD), LANES), jnp.uint32)],
        compiler_params=pltpu.CompilerParams(dimension_semantics=("parallel", "arbitrary"),
                                             vmem_limit_bytes=PEER_VMEM_LIMIT),
        name="peer_dense",
    )(xn, e, g, u, v)


def _ple_kernel(h_ref, peer_ref, p_ref, g_ref, wg_ref, bg_ref, wp_ref, gf_ref, *o_refs, n_prompt_tiles):
    h = h_ref[...] + peer_ref[...]
    xn = _rms(h, g_ref[...]).astype(BF16)
    gate = jax.nn.sigmoid(_dot(xn, wg_ref[...]) + bg_ref[...])
    out = h + _dot(p_ref[...].astype(BF16), wp_ref[...]) * gate
    if len(o_refs) == 1:
        o_refs[0][...] = out
        return
    y = _rms(out, gf_ref[...])
    is_prompt = pl.program_id(0) < n_prompt_tiles

    @pl.when(is_prompt)
    def _():
        o_refs[0][...] = y

    @pl.when(jnp.logical_not(is_prompt))
    def _():
        o_refs[1][...] = y


def _ple(h1, peer_out, p, g, wg, bg, wp, gf, layer, n_prompt, final):
    n = h1.shape[0]
    tm = TOK_TILE
    npt = n_prompt // tm
    row = lambda i: (i, 0)
    fixed = lambda i: (0, 0)
    if final:
        out_specs = [pl.BlockSpec((tm, D_MODEL), lambda i: (jnp.minimum(i, npt - 1), 0)),
                     pl.BlockSpec((tm, D_MODEL), lambda i: (jnp.maximum(i - npt, 0), 0))]
        out_shape = [jax.ShapeDtypeStruct((n_prompt, D_MODEL), F32),
                     jax.ShapeDtypeStruct((n - n_prompt, D_MODEL), F32)]
    else:
        out_specs = [pl.BlockSpec((tm, D_MODEL), row)]
        out_shape = [jax.ShapeDtypeStruct((n, D_MODEL), F32)]
    return pl.pallas_call(
        functools.partial(_ple_kernel, n_prompt_tiles=npt),
        grid=(n // tm,),
        in_specs=[pl.BlockSpec((tm, D_MODEL), row), pl.BlockSpec((tm, D_MODEL), row),
                  pl.BlockSpec((None, tm, PLE_DIM), lambda i: (layer, i, 0)),
                  pl.BlockSpec((1, D_MODEL), fixed), pl.BlockSpec((D_MODEL, D_MODEL), fixed),
                  pl.BlockSpec((1, D_MODEL), fixed), pl.BlockSpec((PLE_DIM, D_MODEL), fixed),
                  pl.BlockSpec((1, D_MODEL), fixed)],
        out_specs=out_specs,
        out_shape=out_shape,
        compiler_params=pltpu.CompilerParams(dimension_semantics=("arbitrary",),
                                             vmem_limit_bytes=VMEM_LIMIT),
        name="ple",
    )(h1, peer_out, p, g, wg, bg, wp, gf)


def kernel(x_prompt, x_sample, cache_k, cache_v, p_prompt, p_sample, g_mix, w_in, g_sb_out, w_spatial, b_spatial, g_a_v, g_a_out, w_out, g_ffn, w_peer_q, peer_sub_keys, peer_u, peer_v, g_ple, w_ple_gate, b_ple_gate, w_ple, g_final):
    depth = w_in.shape[0]
    b, t, d = x_prompt.shape
    bs, ts, _ = x_sample.shape
    past = cache_k.shape[2]
    n_p, n_s = b * t, bs * ts
    assert d == D_MODEL and t % SB_BLOCK == 0 and past % SB_BLOCK == 0 and t % TOK_TILE == 0
    assert n_p % TOK_TILE == 0 and n_s % TOK_TILE == 0 and MIX_L % ts == 0 and n_p % ts == 0
    assert (n_p + n_s) % ROUTE_TILE == 0 and (n_p + n_s) % PEER_TILE == 0

    h = jnp.concatenate([x_prompt.reshape(n_p, d), x_sample.reshape(n_s, d)], axis=0)
    row = lambda a: a.reshape(1, -1)
    p_all = jnp.concatenate([p_prompt.reshape(depth, n_p, PLE_DIM), p_sample.reshape(depth, n_s, PLE_DIM)], axis=1)
    u_tab, v_tab = peer_u.astype(BF16), peer_v.astype(BF16)
    cache_kt = jnp.transpose(cache_k, (0, 1, 3, 4, 2)).reshape(depth, bs, SB_W, past)
    cache_vt = jnp.transpose(cache_v, (0, 1, 3, 4, 2)).reshape(depth, bs, SB_W, past)
    kv = None
    vas = None
    for l in range(depth):
        wkv_t = jnp.transpose(w_in[l][:, SB_W:3 * SB_W]).astype(BF16)
        (q, ktb, vtb, ua, va), kv = _inproj(h, row(g_mix[l]), w_in[l].astype(BF16), wkv_t, l, depth, b, t, kv)
        o = _sb_prompt(q, ktb, vtb, b, t)
        o = _sb_sample(q, kv[2], kv[3], cache_kt, cache_vt, o, l, n_p, ts)
        ws, bsp = _spatial_params(w_spatial[l], b_spatial[l], ts)
        h1, vas = _mixer_out(h, ua, va, o, ws, bsp, row(g_a_v[l]), row(g_a_out[l]), row(g_sb_out[l]),
                             w_out[l].astype(BF16), n_p, l, depth, vas)
        wq = jnp.transpose(w_peer_q[l].reshape(d, PEER_HEADS, LANES), (1, 0, 2)).astype(BF16)
        xn, e, gate = _peer_route(h1, row(g_ffn[l]), wq, _peer_key_matrix(peer_sub_keys[l]))
        peer_out = _peer_dense(xn, e, gate, u_tab, v_tab, l)
        res = _ple(h1, peer_out, p_all, row(g_ple[l]), w_ple_gate[l].astype(BF16), row(b_ple_gate[l]),
                   w_ple[l].astype(BF16), row(g_final), l, n_p, final=(l == depth - 1))
        h = res[0]
    y_p, y_s = res
    k_p, v_p, k_s, v_s = kv

    def untranspose(a):
        return jnp.transpose(a.reshape(depth, b, N_HEADS, HEAD_DIM, t), (0, 1, 4, 2, 3))

    return (y_p.reshape(b, t, d), y_s.reshape(bs, ts, d), untranspose(k_p), untranspose(v_p),
            k_s.reshape(depth, bs, ts, N_HEADS, HEAD_DIM), v_s.reshape(depth, bs, ts, N_HEADS, HEAD_DIM),
            vas.reshape(depth, bs, ts, A_W // GROUP, GROUP))
```

```python
import functools

import jax
import jax.numpy as jnp
from jax import lax
from jax.experimental import pallas as pl
from jax.experimental.pallas import tpu as pltpu

F32 = jnp.float32
BF16 = jnp.bfloat16

LANES = 128
SUBLANES = 8
D_MODEL = 1024
N_HEADS = 8
HEAD_DIM = 64
SB_W = N_HEADS * HEAD_DIM
A_W = 512
IN_W = 3 * SB_W + 2 * A_W
GROUP = 64
MIX_L = 128
PEER_HEADS = 8
PEER_NKEYS = 128
PEER_TOPK = 16
PEER_EXPERTS = PEER_NKEYS * PEER_NKEYS
PLE_DIM = 256
RMS_EPS = 1e-6
TOK_TILE = 512
ROUTE_TILE = SUBLANES * LANES
SB_BLOCK = 256
SB_SAMPLE_STREAMS = 2
SB_STEP_HEADS = 8
PEER_TILE = 512
PEER_CHUNK = 2048
PEER_VMEM_LIMIT = 58 * 1024 * 1024
BUILD_UNROLL = 128
GS_PAD = 8
VMEM_LIMIT = 48 * 1024 * 1024
INV_SQRT2 = 0.7071067811865476


def _dot(a, b):
    return jnp.dot(a, b, preferred_element_type=F32)


def _dot_nt(a, b):
    return lax.dot_general(a, b, (((1,), (1,)), ((), ())), preferred_element_type=F32)


def _split_bf16(x):
    hi = x.astype(BF16)
    lo = (x - hi.astype(F32)).astype(BF16)
    return hi, lo


def _gelu(x):
    return 0.5 * x * (1.0 + lax.erf(x * INV_SQRT2))


def _rms(x, g):
    ms = jnp.mean(x * x, axis=-1, keepdims=True)
    return x * lax.rsqrt(ms + RMS_EPS) * g


def _group_mean_matrix():
    r = lax.broadcasted_iota(jnp.int32, (LANES, LANES), 0) // GROUP
    c = lax.broadcasted_iota(jnp.int32, (LANES, LANES), 1) // GROUP
    return jnp.where(r == c, 1.0 / GROUP, 0.0).astype(BF16)


def _group_rms(x, g, bd):
    xx = x * x
    hi, lo = _split_bf16(xx)
    parts = []
    for j in range(x.shape[1] // LANES):
        sl = slice(LANES * j, LANES * (j + 1))
        parts.append(_dot(hi[:, sl], bd) + _dot(lo[:, sl], bd))
    ms = jnp.concatenate(parts, axis=1)
    return x * lax.rsqrt(ms + RMS_EPS) * g


def _inproj_kernel(xp_ref, xs_ref, g_ref, w_ref, wkv_t_ref, *refs, n_prompt_tiles):
    q_ref, ktb_ref, vtb_ref, ua_ref, va_ref, kp_ref, vp_ref, ks_ref, vs_ref = refs[-9:]
    is_prompt = pl.program_id(0) < n_prompt_tiles
    xn = _rms(jnp.where(is_prompt, xp_ref[...], xs_ref[...]), g_ref[...]).astype(BF16)
    q = _dot(xn, w_ref[:, 0:SB_W])
    q_ref[...] = (q * (HEAD_DIM ** -0.5)).astype(BF16)
    ua_ref[...] = _dot(xn, w_ref[:, 3 * SB_W:3 * SB_W + A_W])
    va_ref[...] = _dot(xn, w_ref[:, 3 * SB_W + A_W:IN_W])
    kvt = _dot_nt(wkv_t_ref[...], xn)
    kt, vt = kvt[0:SB_W], kvt[SB_W:2 * SB_W]
    for j in range(ktb_ref.shape[0]):
        ktb_ref[j] = kt[:, j * SB_BLOCK:(j + 1) * SB_BLOCK].astype(BF16)
        vtb_ref[j] = vt[:, j * SB_BLOCK:(j + 1) * SB_BLOCK].astype(BF16)

    @pl.when(is_prompt)
    def _():
        kp_ref[...] = kt
        vp_ref[...] = vt

    @pl.when(jnp.logical_not(is_prompt))
    def _():
        ks_ref[...] = _dot(xn, w_ref[:, SB_W:2 * SB_W])
        vs_ref[...] = _dot(xn, w_ref[:, 2 * SB_W:3 * SB_W])


def _split_rows_specs(layer, tile, width, n_prompt_tiles):
    prompt = pl.BlockSpec((None, tile, width), lambda i: (layer, jnp.minimum(i, n_prompt_tiles - 1), 0))
    sample = pl.BlockSpec((None, tile, width), lambda i: (layer, jnp.maximum(i - n_prompt_tiles, 0), 0))
    return prompt, sample


def _split_rows_2d(tile, width, n_prompt_tiles):
    return (pl.BlockSpec((tile, width), lambda i: (jnp.minimum(i, n_prompt_tiles - 1), 0)),
            pl.BlockSpec((tile, width), lambda i: (jnp.maximum(i - n_prompt_tiles, 0), 0)))


def _inproj(h_p, h_s, g, w, wkv_t, layer, depth, batch, seq, stacked):
    n = h_p.shape[0] + h_s.shape[0]
    tm = TOK_TILE
    n_prompt = batch * seq
    npt, per_seq = n_prompt // tm, seq // tm
    row = lambda i: (i, 0)
    fixed = lambda i: (0, 0)
    out_f = jax.ShapeDtypeStruct((n, SB_W), F32)
    out_b = jax.ShapeDtypeStruct((n, SB_W), BF16)
    out_t = jax.ShapeDtypeStruct((n // SB_BLOCK, SB_W, SB_BLOCK), BF16)
    stack_p = jax.ShapeDtypeStruct((depth, batch, SB_W, seq), F32)
    stack_s = jax.ShapeDtypeStruct((depth, n - n_prompt, SB_W), F32)
    blk = pl.BlockSpec((tm, SB_W), row)
    blk_t = pl.BlockSpec((tm // SB_BLOCK, SB_W, SB_BLOCK), lambda i: (i, 0, 0))
    _, sample_blk = _split_rows_specs(layer, tm, SB_W, npt)

    def prompt_idx(i):
        t = jnp.minimum(i, npt - 1)
        return (layer, t // per_seq, 0, t % per_seq)

    prompt_blk = pl.BlockSpec((None, None, SB_W, tm), prompt_idx)
    carried = [] if stacked is None else list(stacked)
    outs = pl.pallas_call(
        functools.partial(_inproj_kernel, n_prompt_tiles=npt),
        grid=(n // tm,),
        in_specs=list(_split_rows_2d(tm, D_MODEL, npt))
                 + [pl.BlockSpec((1, D_MODEL), fixed), pl.BlockSpec((D_MODEL, IN_W), fixed),
                    pl.BlockSpec((2 * SB_W, D_MODEL), fixed)]
                 + [pl.BlockSpec(memory_space=pl.ANY)] * len(carried),
        out_specs=[blk, blk_t, blk_t, blk, blk, prompt_blk, prompt_blk, sample_blk, sample_blk],
        out_shape=[out_b, out_t, out_t, out_f, out_f, stack_p, stack_p, stack_s, stack_s],
        input_output_aliases={5 + j: 5 + j for j in range(len(carried))},
        compiler_params=pltpu.CompilerParams(dimension_semantics=("arbitrary",),
                                             vmem_limit_bytes=VMEM_LIMIT),
        name="inproj",
    )(h_p, h_s, g, w, wkv_t, *carried)
    return outs[:5], tuple(outs[5:])


def _strict_lower_ones(n):
    r = lax.broadcasted_iota(jnp.int32, (n, n), 0)
    c = lax.broadcasted_iota(jnp.int32, (n, n), 1)
    return jnp.where(r > c, 1.0, 0.0).astype(BF16)


def _sb_blocks(chains, ustrict, mask):
    zs = [_dot(qh, kt) for qh, kt, _, _ in chains]
    log_betas, log_1ms, splits = [], [], []
    for z in zs:
        sp = jnp.log(1.0 + jnp.exp(-jnp.abs(z)))
        log_beta = jnp.minimum(z, 0.0) - sp
        log_1m = log_beta - z
        if mask is not None:
            log_1m = jnp.where(mask, log_1m, 0.0)
        log_betas.append(log_beta)
        log_1ms.append(log_1m)
        splits.append(_split_bf16(log_1m))
    laters = [_dot(hi, ustrict) + _dot(lo, ustrict) for hi, lo in splits]
    ws, carries = [], []
    for (_, kt, _, carry), log_beta, log_1m, later in zip(chains, log_betas, log_1ms, laters):
        rest = later + jnp.concatenate([carry] * (kt.shape[1] // LANES), axis=1)
        w = jnp.exp(log_beta + rest)
        if mask is not None:
            w = jnp.where(mask, w, 0.0)
        ws.append(w.astype(BF16))
        carries.append(carry + jnp.broadcast_to(later[:, :1] + log_1m[:, :1], carry.shape))
    return [(_dot_nt(w, vth), carry) for w, (_, _, vth, _), carry in zip(ws, chains, carries)]


def _sb_prompt_kernel(q_ref, kt_ref, vt_ref, o_ref, acc_ref, carry_ref):
    tq = q_ref.shape[0]
    tk = tq
    qi = pl.program_id(2)
    ustrict = _strict_lower_ones(tk)
    r = lax.broadcasted_iota(jnp.int32, (tq, tk), 0)
    c = lax.broadcasted_iota(jnp.int32, (tq, tk), 1)
    diag = c < r
    lane_head = lax.broadcasted_iota(jnp.int32, (1, LANES), 1) // HEAD_DIM
    row_head = lax.broadcasted_iota(jnp.int32, (LANES, 1), 0) // HEAD_DIM
    n_pairs = q_ref.shape[1] // LANES
    lanes = [slice(pr * LANES, (pr + 1) * LANES) for pr in range(n_pairs)]
    qhs = [[jnp.where(lane_head == hh, q_ref[:, ln], jnp.zeros((tq, LANES), BF16)) for hh in range(2)]
           for ln in lanes]
    acc_ref[...] = jnp.zeros_like(acc_ref)
    carry_ref[...] = jnp.zeros_like(carry_ref)

    def step(kb, mask):
        chains = []
        for pr, ln in enumerate(lanes):
            kt = kt_ref[kb, ln, :]
            vt = vt_ref[kb, ln, :]
            for hh in range(2):
                vth = jnp.where(row_head == hh, vt, jnp.zeros_like(vt))
                chains.append((qhs[pr][hh], kt, vth, carry_ref[2 * pr + hh]))
        results = _sb_blocks(chains, ustrict, mask)
        for i, (_, carry) in enumerate(results):
            carry_ref[i] = carry
        for pr, ln in enumerate(lanes):
            acc_ref[:, ln] += results[2 * pr][0] + results[2 * pr + 1][0]

    step(qi, diag)

    def body(it, _):
        step(qi - 1 - it, None)
        return 0

    lax.fori_loop(0, qi, body, 0)
    o_ref[...] = acc_ref[...]


def _sb_prompt(q, ktb, vtb, batch, seq):
    tq = SB_BLOCK
    nq = seq // tq
    width = SB_STEP_HEADS * HEAD_DIM
    return pl.pallas_call(
        _sb_prompt_kernel,
        grid=(batch, SB_W // width, nq),
        in_specs=[pl.BlockSpec((tq, width), lambda b, j, i: (b * nq + i, j)),
                  pl.BlockSpec((nq, width, tq), lambda b, j, i: (b, j, 0)),
                  pl.BlockSpec((nq, width, tq), lambda b, j, i: (b, j, 0))],
        out_specs=pl.BlockSpec((tq, width), lambda b, j, i: (b * nq + i, j)),
        out_shape=jax.ShapeDtypeStruct((q.shape[0], SB_W), F32),
        scratch_shapes=[pltpu.VMEM((tq, width), F32), pltpu.VMEM((SB_STEP_HEADS, tq, LANES), F32)],
        compiler_params=pltpu.CompilerParams(
            dimension_semantics=("parallel", "parallel", "arbitrary"), vmem_limit_bytes=VMEM_LIMIT),
        name="sb_prompt",
    )(q, ktb, vtb)


def _sb_sample_kernel(q_ref, k_ref, v_ref, ck_ref, cv_ref, prompt_rows_ref, o_ref, *, ts):
    del prompt_rows_ref
    streams = q_ref.shape[0] // ts
    past = ck_ref.shape[2]
    tk = SB_BLOCK
    ustrict = _strict_lower_ones(tk)
    ustrict_d = ustrict[:LANES, :LANES]
    r = lax.broadcasted_iota(jnp.int32, (ts, LANES), 0)
    c = lax.broadcasted_iota(jnp.int32, (ts, LANES), 1)
    diag = c < r
    lane_head = lax.broadcasted_iota(jnp.int32, (1, LANES), 1) // HEAD_DIM
    row_head = lax.broadcasted_iota(jnp.int32, (LANES, 1), 0) // HEAD_DIM
    pad = jnp.zeros((LANES - ts, LANES), F32)
    n_pairs = q_ref.shape[1] // LANES
    lanes = [slice(pr * LANES, (pr + 1) * LANES) for pr in range(n_pairs)]
    rows = [slice(s * ts, (s + 1) * ts) for s in range(streams)]
    heads = [(s, pr, hh, jnp.where(lane_head == hh, q_ref[rows[s], lanes[pr]], jnp.zeros((ts, LANES), BF16)))
             for s in range(streams) for pr in range(n_pairs) for hh in range(2)]

    def own(hh, vt):
        return jnp.where(row_head == hh, vt, jnp.zeros_like(vt))

    def new_t(ref, s, pr):
        return jnp.transpose(jnp.concatenate([ref[rows[s], lanes[pr]], pad], axis=0)).astype(BF16)

    zero = jnp.zeros((ts, LANES), F32)
    new_k = [[new_t(k_ref, s, pr) for pr in range(n_pairs)] for s in range(streams)]
    new_v = [[new_t(v_ref, s, pr) for pr in range(n_pairs)] for s in range(streams)]
    results = _sb_blocks([(qh, new_k[s][pr], own(hh, new_v[s][pr]), zero) for s, pr, hh, qh in heads],
                         ustrict_d, diag)
    accs = [out for out, _ in results]
    for kb in range(past // tk - 1, -1, -1):
        keys = slice(kb * tk, (kb + 1) * tk)
        old_k = [[ck_ref[s, lanes[pr], keys].astype(BF16) for pr in range(n_pairs)] for s in range(streams)]
        old_v = [[cv_ref[s, lanes[pr], keys].astype(BF16) for pr in range(n_pairs)] for s in range(streams)]
        results = _sb_blocks([(qh, old_k[s][pr], own(hh, old_v[s][pr]), carry)
                              for (s, pr, hh, qh), (_, carry) in zip(heads, results)], ustrict, None)
        accs = [acc + out for acc, (out, _) in zip(accs, results)]
    for s in range(streams):
        for pr in range(n_pairs):
            i = 2 * (s * n_pairs + pr)
            o_ref[rows[s], lanes[pr]] = accs[i] + accs[i + 1]


def _sb_sample(q, k_s, v_s, cache_kt, cache_vt, o, layer, n_prompt, ts):
    _, batch, width, past = cache_kt.shape
    per = SB_SAMPLE_STREAMS
    off = n_prompt // (per * ts)
    rows = pl.BlockSpec((per * ts, SB_W), lambda b: (off + b, 0))
    new = pl.BlockSpec((None, per * ts, SB_W), lambda b: (layer, b, 0))
    old = pl.BlockSpec((None, per, width, past), lambda b: (layer, b, 0, 0))
    return pl.pallas_call(
        functools.partial(_sb_sample_kernel, ts=ts),
        grid=(batch // per,),
        in_specs=[rows, new, new, old, old, pl.BlockSpec(memory_space=pl.ANY)],
        out_specs=rows,
        out_shape=jax.ShapeDtypeStruct(o.shape, o.dtype),
        input_output_aliases={5: 0},
        compiler_params=pltpu.CompilerParams(dimension_semantics=("parallel",),
                                             vmem_limit_bytes=VMEM_LIMIT),
        name="sb_sample",
    )(q, k_s, v_s, cache_kt, cache_vt, o)


def _mixer_out_kernel(hp_ref, hs_ref, ua_ref, va_ref, o_ref, ws_ref, bs_ref, gav_ref, gao_ref, gsb_ref, wo_ref,
                      *refs, n_prompt_tiles):
    h1_ref, vas_ref = refs[-2:]
    tm = ua_ref.shape[0]
    is_prompt = pl.program_id(0) < n_prompt_tiles
    bd = _group_mean_matrix()
    lane_group = lax.broadcasted_iota(jnp.int32, (1, LANES), 1) // GROUP
    u = _gelu(ua_ref[...])
    vn = _group_rms(_gelu(va_ref[...]), gav_ref[...], bd)

    @pl.when(jnp.logical_not(is_prompt))
    def _():
        vas_ref[...] = vn

    vnb = vn.astype(BF16)
    rows = []
    for rr in range(tm // MIX_L):
        slabs = []
        for j in range(A_W // LANES):
            v2 = vnb[rr * MIX_L:(rr + 1) * MIX_L, j * LANES:(j + 1) * LANES]
            mix = jnp.zeros((MIX_L, LANES), F32)
            for gg in range(2):
                v2g = jnp.where(lane_group == gg, v2, jnp.zeros_like(v2))
                mix = mix + _dot(ws_ref[2 * j + gg], v2g)
            slabs.append(mix)
        rows.append(jnp.concatenate(slabs, axis=1) + bs_ref[...])
    mix = jnp.concatenate(rows, axis=0)
    oa = _group_rms(u * mix, gao_ref[...], bd)
    osb = _group_rms(o_ref[...], gsb_ref[...], bd)
    cat = jnp.concatenate([osb, oa], axis=1).astype(BF16)
    h1_ref[...] = jnp.where(is_prompt, hp_ref[...], hs_ref[...]) + _dot(cat, wo_ref[...])


def _mixer_out(h_p, h_s, ua, va, o, ws, bs, gav, gao, gsb, wo, layer, depth, stacked):
    n_prompt = h_p.shape[0]
    n = n_prompt + h_s.shape[0]
    tm = TOK_TILE
    npt = n_prompt // tm
    row = lambda i: (i, 0)
    fixed = lambda i: (0, 0)
    variant = lambda i: jnp.where(i < npt, 0, 1)
    _, sample_blk = _split_rows_specs(layer, tm, A_W, npt)
    carried = [] if stacked is None else [stacked]
    return pl.pallas_call(
        functools.partial(_mixer_out_kernel, n_prompt_tiles=npt),
        grid=(n // tm,),
        in_specs=list(_split_rows_2d(tm, D_MODEL, npt))
                 + [pl.BlockSpec((tm, A_W), row), pl.BlockSpec((tm, A_W), row), pl.BlockSpec((tm, SB_W), row),
                    pl.BlockSpec((None, 2 * A_W // LANES, MIX_L, MIX_L), lambda i: (variant(i), 0, 0, 0)),
                    pl.BlockSpec((None, MIX_L, A_W), lambda i: (variant(i), 0, 0)),
                    pl.BlockSpec((1, A_W), fixed), pl.BlockSpec((1, A_W), fixed),
                    pl.BlockSpec((1, SB_W), fixed), pl.BlockSpec((D_MODEL, D_MODEL), fixed)]
                 + [pl.BlockSpec(memory_space=pl.ANY)] * len(carried),
        out_specs=[pl.BlockSpec((tm, D_MODEL), row), sample_blk],
        out_shape=[jax.ShapeDtypeStruct((n, D_MODEL), F32),
                   jax.ShapeDtypeStruct((depth, n - n_prompt, A_W), F32)],
        input_output_aliases={11: 1} if carried else {},
        compiler_params=pltpu.CompilerParams(dimension_semantics=("arbitrary",),
                                             vmem_limit_bytes=VMEM_LIMIT),
        name="mixer_out",
    )(h_p, h_s, ua, va, o, ws, bs, gav, gao, gsb, wo, *carried)


def _spatial_params(w_s, b_s, ts):
    tril = jnp.tril(jnp.ones((MIX_L, MIX_L), bool))
    w_p = jnp.where(tril[None], w_s, 0.0)
    rep = MIX_L // ts
    w_small = jnp.where(tril[None, :ts, :ts], w_s[:, :ts, :ts], 0.0)
    eye = jnp.eye(rep, dtype=w_s.dtype)
    w_smp = jnp.einsum("ab,gts->gatbs", eye, w_small).reshape(-1, MIX_L, MIX_L)
    b_p = jnp.repeat(jnp.transpose(b_s), A_W // b_s.shape[0], axis=1)
    b_smp = jnp.tile(b_p[:ts], (rep, 1))
    return jnp.stack([w_p, w_smp]).astype(BF16), jnp.stack([b_p, b_smp])


def _argmax_tree(leaf, lo, hi):
    if hi - lo == 1:
        return leaf(lo)
    mid = (lo + hi) // 2
    va, ta = _argmax_tree(leaf, lo, mid)
    vb, tb = _argmax_tree(leaf, mid, hi)
    return jnp.maximum(va, vb), jnp.where(va >= vb, ta, tb)


def _vreg_rows(idx):
    if isinstance(idx, int):
        return slice(idx * SUBLANES, (idx + 1) * SUBLANES)
    return pl.ds(pl.multiple_of(idx * SUBLANES, SUBLANES), SUBLANES)


_PEER_PAIRS = [(i, j) for i in range(PEER_TOPK) for j in range(PEER_TOPK) if (i + 1) * (j + 1) <= PEER_TOPK]


def _peer_route_kernel(h_ref, g_ref, wq_ref, km_ref, xn_ref, e_ref, gate_ref,
                       s_ref, topv_ref, topi_ref, cv_ref, ce_ref, sc_ref, rese_ref, resg_ref):
    k_top = PEER_TOPK
    xn = _rms(h_ref[...], g_ref[...]).astype(BF16)
    xn_ref[...] = xn
    neg_inf = -jnp.inf
    none_yet = jnp.full((SUBLANES, LANES), -1, jnp.int32)

    def head(hd, _):
        pq = _dot(xn, wq_ref[hd]).astype(BF16)
        for c in range(SUBLANES):
            st = _dot_nt(km_ref[hd], pq[c * LANES:(c + 1) * LANES])
            s_ref[pl.ds(c, 2 * PEER_NKEYS, stride=SUBLANES), :] = st

        for p in range(2):
            def extract(k, prev, p=p):
                def leaf(n):
                    r = _vreg_rows(p * PEER_NKEYS + n)
                    v = jnp.where(prev == n, neg_inf, s_ref[r, :])
                    s_ref[r, :] = v
                    return v, n
                m, im = _argmax_tree(leaf, 0, PEER_NKEYS)
                topv_ref[_vreg_rows(p * k_top + k), :] = m
                topi_ref[_vreg_rows(p * k_top + k), :] = im
                return im
            lax.fori_loop(0, k_top, extract, none_yet)

        for idx, (i, j) in enumerate(_PEER_PAIRS):
            cv_ref[_vreg_rows(idx), :] = topv_ref[_vreg_rows(i), :] + topv_ref[_vreg_rows(k_top + j), :]
            ce_ref[_vreg_rows(idx), :] = (topi_ref[_vreg_rows(i), :] * PEER_NKEYS
                                          + topi_ref[_vreg_rows(k_top + j), :])

        def pick(k, prev):
            def leaf(idx):
                r = _vreg_rows(idx)
                eid = ce_ref[r, :]
                v = jnp.where(eid == prev, neg_inf, cv_ref[r, :])
                cv_ref[r, :] = v
                return v, eid
            m, em = _argmax_tree(leaf, 0, len(_PEER_PAIRS))
            sc_ref[_vreg_rows(k), :] = m
            rese_ref[_vreg_rows(hd * k_top + k), :] = em.astype(F32)
            return em
        lax.fori_loop(0, k_top, pick, none_yet)

        mx = sc_ref[_vreg_rows(0), :]
        exs = [jnp.exp(sc_ref[_vreg_rows(k), :] - mx) for k in range(k_top)]
        total = exs[0]
        for ex in exs[1:]:
            total = total + ex
        for k in range(k_top):
            resg_ref[_vreg_rows(hd * k_top + k), :] = exs[k] / total
        return 0

    lax.fori_loop(0, PEER_HEADS, head, 0)
    for c in range(SUBLANES):
        rows = slice(c * LANES, (c + 1) * LANES)
        e_ref[rows, :] = jnp.transpose(rese_ref[pl.ds(c, LANES, stride=SUBLANES), :]).astype(jnp.int32)
        gate_ref[rows, :] = jnp.transpose(resg_ref[pl.ds(c, LANES, stride=SUBLANES), :])


def _peer_route(h1, g, wq, km):
    n = h1.shape[0]
    tm = ROUTE_TILE
    row = lambda i: (i, 0)
    vregs = lambda count, dtype: pltpu.VMEM((count * SUBLANES, LANES), dtype)
    return pl.pallas_call(
        _peer_route_kernel,
        grid=(n // tm,),
        in_specs=[pl.BlockSpec((tm, D_MODEL), row), pl.BlockSpec((1, D_MODEL), lambda i: (0, 0)),
                  pl.BlockSpec((PEER_HEADS, D_MODEL, LANES), lambda i: (0, 0, 0)),
                  pl.BlockSpec((PEER_HEADS, 2 * PEER_NKEYS, LANES), lambda i: (0, 0, 0))],
        out_specs=[pl.BlockSpec((tm, D_MODEL), row), pl.BlockSpec((tm, LANES), row),
                   pl.BlockSpec((tm, LANES), row)],
        out_shape=[jax.ShapeDtypeStruct((n, D_MODEL), BF16),
                   jax.ShapeDtypeStruct((n, LANES), jnp.int32),
                   jax.ShapeDtypeStruct((n, LANES), F32)],
        scratch_shapes=[vregs(2 * PEER_NKEYS, F32), vregs(2 * PEER_TOPK, F32), vregs(2 * PEER_TOPK, jnp.int32),
                        vregs(len(_PEER_PAIRS), F32), vregs(len(_PEER_PAIRS), jnp.int32),
                        vregs(PEER_TOPK, F32), vregs(PEER_HEADS * PEER_TOPK, F32),
                        vregs(PEER_HEADS * PEER_TOPK, F32)],
        compiler_params=pltpu.CompilerParams(dimension_semantics=("parallel",),
                                             vmem_limit_bytes=VMEM_LIMIT),
        name="peer_route",
    )(h1, g, wq, km)


def _peer_key_matrix(sub_keys):
    z = jnp.zeros_like(sub_keys[:, 0])
    k0 = jnp.concatenate([sub_keys[:, 0], z], axis=2)
    k1 = jnp.concatenate([z, sub_keys[:, 1]], axis=2)
    return jnp.concatenate([k0, k1], axis=1).astype(BF16)


def _peer_dense_kernel(xn_ref, e_ref, g_ref, u_ref, v_ref, o_ref, gs_ref):
    tt = xn_ref.shape[0]
    ce = u_ref.shape[0]
    stride = tt + GS_PAD
    c = pl.program_id(1)

    @pl.when(c == 0)
    def _():
        o_ref[...] = jnp.zeros_like(o_ref)
        sub = lax.broadcasted_iota(jnp.int32, (PEER_NKEYS, LANES), 0)

        def tok(t, _):
            e_row = e_ref[pl.ds(t, 1), :]
            g_row = g_ref[pl.ds(t, 1), :]
            left = jnp.where(sub == e_row // PEER_NKEYS, g_row, 0.0).astype(BF16)
            right = jnp.where(sub == e_row % PEER_NKEYS, 1.0, 0.0).astype(BF16)
            gmap = _dot_nt(left, right).astype(BF16)
            gs_ref[pl.ds(t, PEER_NKEYS // 2, stride=stride), :] = pltpu.bitcast(gmap, jnp.uint32)
            return 0

        lax.fori_loop(0, tt, tok, 0, unroll=BUILD_UNROLL)

    hid = _dot_nt(xn_ref[...], u_ref[...])
    slabs = []
    for j in range(ce // LANES // 2):
        start = pl.multiple_of((c * (ce // LANES // 2) + j) * stride, SUBLANES)
        packed = gs_ref[pl.ds(start, tt), :]
        slabs.append(lax.bitcast_convert_type(packed << 16, F32))
        slabs.append(lax.bitcast_convert_type(packed & jnp.uint32(0xFFFF0000), F32))
    gate = jnp.concatenate(slabs, axis=1)
    a = (gate * _gelu(hid)).astype(BF16)
    o_ref[...] += _dot(a, v_ref[...])


def _peer_dense(xn, e, g, u, v, layer):
    n = xn.shape[0]
    tt = PEER_TILE
    ce = PEER_CHUNK
    row = lambda i, c: (i, 0)
    chunk = lambda i, c: (layer, c, 0)
    return pl.pallas_call(
        _peer_dense_kernel,
        grid=(n // tt, PEER_EXPERTS // ce),
        in_specs=[pl.BlockSpec((tt, D_MODEL), row), pl.BlockSpec((tt, LANES), row),
                  pl.BlockSpec((tt, LANES), row), pl.BlockSpec((None, ce, D_MODEL), chunk),
                  pl.BlockSpec((None, ce, D_MODEL), chunk)],
        out_specs=pl.BlockSpec((tt, D_MODEL), row),
        out_shape=jax.ShapeDtypeStruct((n, D_MODEL), F32),
        scratch_shapes=[pltpu.VMEM((PEER_NKEYS // 2 * (tt + GS_PAD), LANES), jnp.uint32)],
        compiler_params=pltpu.CompilerParams(dimension_semantics=("parallel", "arbitrary"),
                                             vmem_limit_bytes=PEER_VMEM_LIMIT),
        name="peer_dense",
    )(xn, e, g, u, v)


def _ple_kernel(h_ref, peer_ref, pp_ref, ps_ref, g_ref, wg_ref, bg_ref, wp_ref, gf_ref, op_ref, os_ref, *,
                n_prompt_tiles, final):
    is_prompt = pl.program_id(0) < n_prompt_tiles
    h = h_ref[...] + peer_ref[...]
    xn = _rms(h, g_ref[...]).astype(BF16)
    gate = jax.nn.sigmoid(_dot(xn, wg_ref[...]) + bg_ref[...])
    p = jnp.where(is_prompt, pp_ref[...], ps_ref[...])
    out = h + _dot(p.astype(BF16), wp_ref[...]) * gate
    if final:
        out = _rms(out, gf_ref[...])

    @pl.when(is_prompt)
    def _():
        op_ref[...] = out

    @pl.when(jnp.logical_not(is_prompt))
    def _():
        os_ref[...] = out


def _ple(h1, peer_out, p_prompt, p_sample, g, wg, bg, wp, gf, layer, final):
    n = h1.shape[0]
    n_prompt = p_prompt.shape[1]
    tm = TOK_TILE
    npt = n_prompt // tm
    row = lambda i: (i, 0)
    fixed = lambda i: (0, 0)
    pp_blk, ps_blk = _split_rows_specs(layer, tm, PLE_DIM, npt)
    return pl.pallas_call(
        functools.partial(_ple_kernel, n_prompt_tiles=npt, final=final),
        grid=(n // tm,),
        in_specs=[pl.BlockSpec((tm, D_MODEL), row), pl.BlockSpec((tm, D_MODEL), row), pp_blk, ps_blk,
                  pl.BlockSpec((1, D_MODEL), fixed), pl.BlockSpec((D_MODEL, D_MODEL), fixed),
                  pl.BlockSpec((1, D_MODEL), fixed), pl.BlockSpec((PLE_DIM, D_MODEL), fixed),
                  pl.BlockSpec((1, D_MODEL), fixed)],
        out_specs=list(_split_rows_2d(tm, D_MODEL, npt)),
        out_shape=[jax.ShapeDtypeStruct((n_prompt, D_MODEL), F32),
                   jax.ShapeDtypeStruct((n - n_prompt, D_MODEL), F32)],
        compiler_params=pltpu.CompilerParams(dimension_semantics=("arbitrary",),
                                             vmem_limit_bytes=VMEM_LIMIT),
        name="ple",
    )(h1, peer_out, p_prompt, p_sample, g, wg, bg, wp, gf)


def kernel(x_prompt, x_sample, cache_k, cache_v, p_prompt, p_sample, g_mix, w_in, g_sb_out, w_spatial, b_spatial, g_a_v, g_a_out, w_out, g_ffn, w_peer_q, peer_sub_keys, peer_u, peer_v, g_ple, w_ple_gate, b_ple_gate, w_ple, g_final):
    depth = w_in.shape[0]
    b, t, d = x_prompt.shape
    bs, ts, _ = x_sample.shape
    past = cache_k.shape[2]
    n_p, n_s = b * t, bs * ts
    assert d == D_MODEL and t % SB_BLOCK == 0 and past % SB_BLOCK == 0 and t % TOK_TILE == 0
    assert n_p % TOK_TILE == 0 and n_s % TOK_TILE == 0 and MIX_L % ts == 0 and n_p % ts == 0
    assert (n_p + n_s) % ROUTE_TILE == 0 and (n_p + n_s) % PEER_TILE == 0
    assert bs % SB_SAMPLE_STREAMS == 0 and n_p % (SB_SAMPLE_STREAMS * ts) == 0

    h_p, h_s = x_prompt.reshape(n_p, d), x_sample.reshape(n_s, d)
    row = lambda a: a.reshape(1, -1)
    pp, ps = p_prompt.reshape(depth, n_p, PLE_DIM), p_sample.reshape(depth, n_s, PLE_DIM)
    u_tab, v_tab = peer_u.astype(BF16), peer_v.astype(BF16)
    cache_kt = jnp.transpose(cache_k, (0, 1, 3, 4, 2)).reshape(depth, bs, SB_W, past)
    cache_vt = jnp.transpose(cache_v, (0, 1, 3, 4, 2)).reshape(depth, bs, SB_W, past)
    kv = None
    vas = None
    for l in range(depth):
        wkv_t = jnp.transpose(w_in[l][:, SB_W:3 * SB_W]).astype(BF16)
        (q, ktb, vtb, ua, va), kv = _inproj(h_p, h_s, row(g_mix[l]), w_in[l].astype(BF16), wkv_t, l, depth, b, t, kv)
        o = _sb_prompt(q, ktb, vtb, b, t)
        o = _sb_sample(q, kv[2], kv[3], cache_kt, cache_vt, o, l, n_p, ts)
        ws, bsp = _spatial_params(w_spatial[l], b_spatial[l], ts)
        h1, vas = _mixer_out(h_p, h_s, ua, va, o, ws, bsp, row(g_a_v[l]), row(g_a_out[l]), row(g_sb_out[l]),
                             w_out[l].astype(BF16), l, depth, vas)
        wq = jnp.transpose(w_peer_q[l].reshape(d, PEER_HEADS, LANES), (1, 0, 2)).astype(BF16)
        xn, e, gate = _peer_route(h1, row(g_ffn[l]), wq, _peer_key_matrix(peer_sub_keys[l]))
        peer_out = _peer_dense(xn, e, gate, u_tab, v_tab, l)
        h_p, h_s = _ple(h1, peer_out, pp, ps, row(g_ple[l]), w_ple_gate[l].astype(BF16), row(b_ple_gate[l]),
                        w_ple[l].astype(BF16), row(g_final), l, final=(l == depth - 1))
    y_p, y_s = h_p, h_s
    k_p, v_p, k_s, v_s = kv

    def untranspose(a):
        return jnp.transpose(a.reshape(depth, b, N_HEADS, HEAD_DIM, t), (0, 1, 4, 2, 3))

    return (y_p.reshape(b, t, d), y_s.reshape(bs, ts, d), untranspose(k_p), untranspose(v_p),
            k_s.reshape(depth, bs, ts, N_HEADS, HEAD_DIM), v_s.reshape(depth, bs, ts, N_HEADS, HEAD_DIM),
            vas.reshape(depth, bs, ts, A_W // GROUP, GROUP))
```

```python
import functools

import jax
import jax.numpy as jnp
from jax import lax
from jax.experimental import pallas as pl
from jax.experimental.pallas import tpu as pltpu

F32 = jnp.float32
BF16 = jnp.bfloat16

LANES = 128
SUBLANES = 8
D_MODEL = 1024
N_HEADS = 8
HEAD_DIM = 64
SB_W = N_HEADS * HEAD_DIM
A_W = 512
IN_W = 3 * SB_W + 2 * A_W
GROUP = 64
MIX_L = 128
PEER_HEADS = 8
PEER_NKEYS = 128
PEER_TOPK = 16
PEER_EXPERTS = PEER_NKEYS * PEER_NKEYS
PLE_DIM = 256
RMS_EPS = 1e-6
TOK_TILE = 512
ROUTE_TILE = SUBLANES * LANES
SB_BLOCK = 256
SB_SAMPLE_STREAMS = 2
SB_STEP_HEADS = 8
PEER_TILE = 512
PEER_CHUNK = 2048
PEER_VMEM_LIMIT = 58 * 1024 * 1024
BUILD_UNROLL = 128
GS_PAD = 8
VMEM_LIMIT = 48 * 1024 * 1024
INV_SQRT2 = 0.7071067811865476


def _dot(a, b):
    return jnp.dot(a, b, preferred_element_type=F32)


def _dot_nt(a, b):
    return lax.dot_general(a, b, (((1,), (1,)), ((), ())), preferred_element_type=F32)


def _split_bf16(x):
    hi = x.astype(BF16)
    lo = (x - hi.astype(F32)).astype(BF16)
    return hi, lo


def _gelu(x):
    return 0.5 * x * (1.0 + lax.erf(x * INV_SQRT2))


def _rms(x, g):
    ms = jnp.mean(x * x, axis=-1, keepdims=True)
    return x * lax.rsqrt(ms + RMS_EPS) * g


def _group_mean_matrix():
    r = lax.broadcasted_iota(jnp.int32, (LANES, LANES), 0) // GROUP
    c = lax.broadcasted_iota(jnp.int32, (LANES, LANES), 1) // GROUP
    return jnp.where(r == c, 1.0 / GROUP, 0.0).astype(BF16)


def _group_rms(x, g, bd):
    xx = x * x
    hi, lo = _split_bf16(xx)
    parts = []
    for j in range(x.shape[1] // LANES):
        sl = slice(LANES * j, LANES * (j + 1))
        parts.append(_dot(hi[:, sl], bd) + _dot(lo[:, sl], bd))
    ms = jnp.concatenate(parts, axis=1)
    return x * lax.rsqrt(ms + RMS_EPS) * g


def _inproj_kernel(xp_ref, xs_ref, g_ref, w_ref, wkv_t_ref, *refs, n_prompt_tiles):
    q_ref, ktb_ref, vtb_ref, ua_ref, va_ref, kp_ref, vp_ref, ks_ref, vs_ref = refs[-9:]
    is_prompt = pl.program_id(0) < n_prompt_tiles
    xn = _rms(jnp.where(is_prompt, xp_ref[...], xs_ref[...]), g_ref[...]).astype(BF16)
    q = _dot(xn, w_ref[:, 0:SB_W])
    q_ref[...] = (q * (HEAD_DIM ** -0.5)).astype(BF16)
    ua_ref[...] = _dot(xn, w_ref[:, 3 * SB_W:3 * SB_W + A_W])
    va_ref[...] = _dot(xn, w_ref[:, 3 * SB_W + A_W:IN_W])
    kvt = _dot_nt(wkv_t_ref[...], xn)
    kt, vt = kvt[0:SB_W], kvt[SB_W:2 * SB_W]
    for j in range(ktb_ref.shape[0]):
        ktb_ref[j] = kt[:, j * SB_BLOCK:(j + 1) * SB_BLOCK].astype(BF16)
        vtb_ref[j] = vt[:, j * SB_BLOCK:(j + 1) * SB_BLOCK].astype(BF16)

    @pl.when(is_prompt)
    def _():
        kp_ref[...] = kt
        vp_ref[...] = vt

    @pl.when(jnp.logical_not(is_prompt))
    def _():
        ks_ref[...] = _dot(xn, w_ref[:, SB_W:2 * SB_W])
        vs_ref[...] = _dot(xn, w_ref[:, 2 * SB_W:3 * SB_W])


def _split_rows_specs(layer, tile, width, n_prompt_tiles):
    prompt = pl.BlockSpec((None, tile, width), lambda i: (layer, jnp.minimum(i, n_prompt_tiles - 1), 0))
    sample = pl.BlockSpec((None, tile, width), lambda i: (layer, jnp.maximum(i - n_prompt_tiles, 0), 0))
    return prompt, sample


def _split_rows_2d(tile, width, n_prompt_tiles):
    return (pl.BlockSpec((tile, width), lambda i: (jnp.minimum(i, n_prompt_tiles - 1), 0)),
            pl.BlockSpec((tile, width), lambda i: (jnp.maximum(i - n_prompt_tiles, 0), 0)))


def _inproj(h_p, h_s, g, w, wkv_t, layer, depth, batch, seq, stacked):
    n = h_p.shape[0] + h_s.shape[0]
    tm = TOK_TILE
    n_prompt = batch * seq
    npt, per_seq = n_prompt // tm, seq // tm
    row = lambda i: (i, 0)
    fixed = lambda i: (0, 0)
    out_f = jax.ShapeDtypeStruct((n, SB_W), F32)
    out_b = jax.ShapeDtypeStruct((n, SB_W), BF16)
    out_t = jax.ShapeDtypeStruct((n // SB_BLOCK, SB_W, SB_BLOCK), BF16)
    stack_p = jax.ShapeDtypeStruct((depth, batch, SB_W, seq), F32)
    stack_s = jax.ShapeDtypeStruct((depth, n - n_prompt, SB_W), F32)
    blk = pl.BlockSpec((tm, SB_W), row)
    blk_t = pl.BlockSpec((tm // SB_BLOCK, SB_W, SB_BLOCK), lambda i: (i, 0, 0))
    _, sample_blk = _split_rows_specs(layer, tm, SB_W, npt)

    def prompt_idx(i):
        t = jnp.minimum(i, npt - 1)
        return (layer, t // per_seq, 0, t % per_seq)

    prompt_blk = pl.BlockSpec((None, None, SB_W, tm), prompt_idx)
    carried = [] if stacked is None else list(stacked)
    outs = pl.pallas_call(
        functools.partial(_inproj_kernel, n_prompt_tiles=npt),
        grid=(n // tm,),
        in_specs=list(_split_rows_2d(tm, D_MODEL, npt))
                 + [pl.BlockSpec((1, D_MODEL), fixed), pl.BlockSpec((D_MODEL, IN_W), fixed),
                    pl.BlockSpec((2 * SB_W, D_MODEL), fixed)]
                 + [pl.BlockSpec(memory_space=pl.ANY)] * len(carried),
        out_specs=[blk, blk_t, blk_t, blk, blk, prompt_blk, prompt_blk, sample_blk, sample_blk],
        out_shape=[out_b, out_t, out_t, out_f, out_f, stack_p, stack_p, stack_s, stack_s],
        input_output_aliases={5 + j: 5 + j for j in range(len(carried))},
        compiler_params=pltpu.CompilerParams(dimension_semantics=("arbitrary",),
                                             vmem_limit_bytes=VMEM_LIMIT),
        name="inproj",
    )(h_p, h_s, g, w, wkv_t, *carried)
    return outs[:5], tuple(outs[5:])


def _strict_lower_ones(n):
    r = lax.broadcasted_iota(jnp.int32, (n, n), 0)
    c = lax.broadcasted_iota(jnp.int32, (n, n), 1)
    return jnp.where(r > c, 1.0, 0.0).astype(BF16)


def _sb_blocks(chains, ustrict, mask):
    zs = [_dot(qh, kt) for qh, kt, _, _ in chains]
    log_betas, log_1ms, splits = [], [], []
    for z in zs:
        sp = jnp.log(1.0 + jnp.exp(-jnp.abs(z)))
        log_beta = jnp.minimum(z, 0.0) - sp
        log_1m = log_beta - z
        if mask is not None:
            log_1m = jnp.where(mask, log_1m, 0.0)
        log_betas.append(log_beta)
        log_1ms.append(log_1m)
        splits.append(_split_bf16(log_1m))
    laters = [_dot(hi, ustrict) + _dot(lo, ustrict) for hi, lo in splits]
    ws, carries = [], []
    for (_, kt, _, carry), log_beta, log_1m, later in zip(chains, log_betas, log_1ms, laters):
        rest = later + jnp.concatenate([carry] * (kt.shape[1] // LANES), axis=1)
        w = jnp.exp(log_beta + rest)
        if mask is not None:
            w = jnp.where(mask, w, 0.0)
        ws.append(w.astype(BF16))
        carries.append(carry + jnp.broadcast_to(later[:, :1] + log_1m[:, :1], carry.shape))
    return [(_dot_nt(w, vth), carry) for w, (_, _, vth, _), carry in zip(ws, chains, carries)]


def _sb_prompt_kernel(q_ref, kt_ref, vt_ref, o_ref, acc_ref, carry_ref):
    tq = q_ref.shape[0]
    tk = tq
    qi = pl.program_id(2)
    ustrict = _strict_lower_ones(tk)
    r = lax.broadcasted_iota(jnp.int32, (tq, tk), 0)
    c = lax.broadcasted_iota(jnp.int32, (tq, tk), 1)
    diag = c < r
    lane_head = lax.broadcasted_iota(jnp.int32, (1, LANES), 1) // HEAD_DIM
    row_head = lax.broadcasted_iota(jnp.int32, (LANES, 1), 0) // HEAD_DIM
    n_pairs = q_ref.shape[1] // LANES
    lanes = [slice(pr * LANES, (pr + 1) * LANES) for pr in range(n_pairs)]
    qhs = [[jnp.where(lane_head == hh, q_ref[:, ln], jnp.zeros((tq, LANES), BF16)) for hh in range(2)]
           for ln in lanes]
    acc_ref[...] = jnp.zeros_like(acc_ref)
    carry_ref[...] = jnp.zeros_like(carry_ref)

    def step(kb, mask):
        chains = []
        for pr, ln in enumerate(lanes):
            kt = kt_ref[kb, ln, :]
            vt = vt_ref[kb, ln, :]
            for hh in range(2):
                vth = jnp.where(row_head == hh, vt, jnp.zeros_like(vt))
                chains.append((qhs[pr][hh], kt, vth, carry_ref[2 * pr + hh]))
        results = _sb_blocks(chains, ustrict, mask)
        for i, (_, carry) in enumerate(results):
            carry_ref[i] = carry
        for pr, ln in enumerate(lanes):
            acc_ref[:, ln] += results[2 * pr][0] + results[2 * pr + 1][0]

    step(qi, diag)

    def body(it, _):
        step(qi - 1 - it, None)
        return 0

    lax.fori_loop(0, qi, body, 0)
    o_ref[...] = acc_ref[...]


def _sb_prompt(q, ktb, vtb, batch, seq):
    tq = SB_BLOCK
    nq = seq // tq
    width = SB_STEP_HEADS * HEAD_DIM
    return pl.pallas_call(
        _sb_prompt_kernel,
        grid=(batch, SB_W // width, nq),
        in_specs=[pl.BlockSpec((tq, width), lambda b, j, i: (b * nq + i, j)),
                  pl.BlockSpec((nq, width, tq), lambda b, j, i: (b, j, 0)),
                  pl.BlockSpec((nq, width, tq), lambda b, j, i: (b, j, 0))],
        out_specs=pl.BlockSpec((tq, width), lambda b, j, i: (b * nq + i, j)),
        out_shape=jax.ShapeDtypeStruct((q.shape[0], SB_W), F32),
        scratch_shapes=[pltpu.VMEM((tq, width), F32), pltpu.VMEM((SB_STEP_HEADS, tq, LANES), F32)],
        compiler_params=pltpu.CompilerParams(
            dimension_semantics=("parallel", "parallel", "arbitrary"), vmem_limit_bytes=VMEM_LIMIT),
        name="sb_prompt",
    )(q, ktb, vtb)


def _sb_sample_kernel(q_ref, k_ref, v_ref, ck_ref, cv_ref, prompt_rows_ref, o_ref, *, ts):
    del prompt_rows_ref
    streams = q_ref.shape[0] // ts
    past = ck_ref.shape[2]
    tk = SB_BLOCK
    ustrict = _strict_lower_ones(tk)
    ustrict_d = ustrict[:LANES, :LANES]
    r = lax.broadcasted_iota(jnp.int32, (ts, LANES), 0)
    c = lax.broadcasted_iota(jnp.int32, (ts, LANES), 1)
    diag = c < r
    lane_head = lax.broadcasted_iota(jnp.int32, (1, LANES), 1) // HEAD_DIM
    row_head = lax.broadcasted_iota(jnp.int32, (LANES, 1), 0) // HEAD_DIM
    pad = jnp.zeros((LANES - ts, LANES), F32)
    n_pairs = q_ref.shape[1] // LANES
    lanes = [slice(pr * LANES, (pr + 1) * LANES) for pr in range(n_pairs)]
    rows = [slice(s * ts, (s + 1) * ts) for s in range(streams)]
    heads = [(s, pr, hh, jnp.where(lane_head == hh, q_ref[rows[s], lanes[pr]], jnp.zeros((ts, LANES), BF16)))
             for s in range(streams) for pr in range(n_pairs) for hh in range(2)]

    def own(hh, vt):
        return jnp.where(row_head == hh, vt, jnp.zeros_like(vt))

    def new_t(ref, s, pr):
        return jnp.transpose(jnp.concatenate([ref[rows[s], lanes[pr]], pad], axis=0)).astype(BF16)

    zero = jnp.zeros((ts, LANES), F32)
    new_k = [[new_t(k_ref, s, pr) for pr in range(n_pairs)] for s in range(streams)]
    new_v = [[new_t(v_ref, s, pr) for pr in range(n_pairs)] for s in range(streams)]
    results = _sb_blocks([(qh, new_k[s][pr], own(hh, new_v[s][pr]), zero) for s, pr, hh, qh in heads],
                         ustrict_d, diag)
    accs = [out for out, _ in results]
    for kb in range(past // tk - 1, -1, -1):
        keys = slice(kb * tk, (kb + 1) * tk)
        old_k = [[ck_ref[s, lanes[pr], keys].astype(BF16) for pr in range(n_pairs)] for s in range(streams)]
        old_v = [[cv_ref[s, lanes[pr], keys].astype(BF16) for pr in range(n_pairs)] for s in range(streams)]
        results = _sb_blocks([(qh, old_k[s][pr], own(hh, old_v[s][pr]), carry)
                              for (s, pr, hh, qh), (_, carry) in zip(heads, results)], ustrict, None)
        accs = [acc + out for acc, (out, _) in zip(accs, results)]
    for s in range(streams):
        for pr in range(n_pairs):
            i = 2 * (s * n_pairs + pr)
            o_ref[rows[s], lanes[pr]] = accs[i] + accs[i + 1]


def _sb_sample(q, k_s, v_s, cache_kt, cache_vt, o, layer, n_prompt, ts):
    _, batch, width, past = cache_kt.shape
    per = SB_SAMPLE_STREAMS
    off = n_prompt // (per * ts)
    rows = pl.BlockSpec((per * ts, SB_W), lambda b: (off + b, 0))
    new = pl.BlockSpec((None, per * ts, SB_W), lambda b: (layer, b, 0))
    old = pl.BlockSpec((None, per, width, past), lambda b: (layer, b, 0, 0))
    return pl.pallas_call(
        functools.partial(_sb_sample_kernel, ts=ts),
        grid=(batch // per,),
        in_specs=[rows, new, new, old, old, pl.BlockSpec(memory_space=pl.ANY)],
        out_specs=rows,
        out_shape=jax.ShapeDtypeStruct(o.shape, o.dtype),
        input_output_aliases={5: 0},
        compiler_params=pltpu.CompilerParams(dimension_semantics=("parallel",),
                                             vmem_limit_bytes=VMEM_LIMIT),
        name="sb_sample",
    )(q, k_s, v_s, cache_kt, cache_vt, o)


def _mixer_out_kernel(hp_ref, hs_ref, ua_ref, va_ref, o_ref, ws_ref, bs_ref, gav_ref, gao_ref, gsb_ref, wo_ref,
                      *refs, n_prompt_tiles):
    h1_ref, vas_ref = refs[-2:]
    tm = ua_ref.shape[0]
    is_prompt = pl.program_id(0) < n_prompt_tiles
    bd = _group_mean_matrix()
    lane_group = lax.broadcasted_iota(jnp.int32, (1, LANES), 1) // GROUP
    u = _gelu(ua_ref[...])
    vn = _group_rms(_gelu(va_ref[...]), gav_ref[...], bd)

    @pl.when(jnp.logical_not(is_prompt))
    def _():
        vas_ref[...] = vn

    vnb = vn.astype(BF16)
    rows = []
    for rr in range(tm // MIX_L):
        slabs = []
        for j in range(A_W // LANES):
            v2 = vnb[rr * MIX_L:(rr + 1) * MIX_L, j * LANES:(j + 1) * LANES]
            mix = jnp.zeros((MIX_L, LANES), F32)
            for gg in range(2):
                v2g = jnp.where(lane_group == gg, v2, jnp.zeros_like(v2))
                mix = mix + _dot(ws_ref[2 * j + gg], v2g)
            slabs.append(mix)
        rows.append(jnp.concatenate(slabs, axis=1) + bs_ref[...])
    mix = jnp.concatenate(rows, axis=0)
    oa = _group_rms(u * mix, gao_ref[...], bd)
    osb = _group_rms(o_ref[...], gsb_ref[...], bd)
    cat = jnp.concatenate([osb, oa], axis=1).astype(BF16)
    h1_ref[...] = jnp.where(is_prompt, hp_ref[...], hs_ref[...]) + _dot(cat, wo_ref[...])


def _mixer_out(h_p, h_s, ua, va, o, ws, bs, gav, gao, gsb, wo, layer, depth, stacked):
    n_prompt = h_p.shape[0]
    n = n_prompt + h_s.shape[0]
    tm = TOK_TILE
    npt = n_prompt // tm
    row = lambda i: (i, 0)
    fixed = lambda i: (0, 0)
    variant = lambda i: jnp.where(i < npt, 0, 1)
    _, sample_blk = _split_rows_specs(layer, tm, A_W, npt)
    carried = [] if stacked is None else [stacked]
    return pl.pallas_call(
        functools.partial(_mixer_out_kernel, n_prompt_tiles=npt),
        grid=(n // tm,),
        in_specs=list(_split_rows_2d(tm, D_MODEL, npt))
                 + [pl.BlockSpec((tm, A_W), row), pl.BlockSpec((tm, A_W), row), pl.BlockSpec((tm, SB_W), row),
                    pl.BlockSpec((None, 2 * A_W // LANES, MIX_L, MIX_L), lambda i: (variant(i), 0, 0, 0)),
                    pl.BlockSpec((None, MIX_L, A_W), lambda i: (variant(i), 0, 0)),
                    pl.BlockSpec((1, A_W), fixed), pl.BlockSpec((1, A_W), fixed),
                    pl.BlockSpec((1, SB_W), fixed), pl.BlockSpec((D_MODEL, D_MODEL), fixed)]
                 + [pl.BlockSpec(memory_space=pl.ANY)] * len(carried),
        out_specs=[pl.BlockSpec((tm, D_MODEL), row), sample_blk],
        out_shape=[jax.ShapeDtypeStruct((n, D_MODEL), F32),
                   jax.ShapeDtypeStruct((depth, n - n_prompt, A_W), F32)],
        input_output_aliases={11: 1} if carried else {},
        compiler_params=pltpu.CompilerParams(dimension_semantics=("arbitrary",),
                                             vmem_limit_bytes=VMEM_LIMIT),
        name="mixer_out",
    )(h_p, h_s, ua, va, o, ws, bs, gav, gao, gsb, wo, *carried)


def _spatial_params(w_s, b_s, ts):
    tril = jnp.tril(jnp.ones((MIX_L, MIX_L), bool))
    w_p = jnp.where(tril[None], w_s, 0.0)
    rep = MIX_L // ts
    w_small = jnp.where(tril[None, :ts, :ts], w_s[:, :ts, :ts], 0.0)
    eye = jnp.eye(rep, dtype=w_s.dtype)
    w_smp = jnp.einsum("ab,gts->gatbs", eye, w_small).reshape(-1, MIX_L, MIX_L)
    b_p = jnp.repeat(jnp.transpose(b_s), A_W // b_s.shape[0], axis=1)
    b_smp = jnp.tile(b_p[:ts], (rep, 1))
    return jnp.stack([w_p, w_smp]).astype(BF16), jnp.stack([b_p, b_smp])


def _argmax_tree(leaf, lo, hi):
    if hi - lo == 1:
        return leaf(lo)
    mid = (lo + hi) // 2
    va, ta = _argmax_tree(leaf, lo, mid)
    vb, tb = _argmax_tree(leaf, mid, hi)
    return jnp.maximum(va, vb), jnp.where(va >= vb, ta, tb)


def _vreg_rows(idx):
    if isinstance(idx, int):
        return slice(idx * SUBLANES, (idx + 1) * SUBLANES)
    return pl.ds(pl.multiple_of(idx * SUBLANES, SUBLANES), SUBLANES)


_PEER_PAIRS = [(i, j) for i in range(PEER_TOPK) for j in range(PEER_TOPK) if (i + 1) * (j + 1) <= PEER_TOPK]


def _peer_route_kernel(h_ref, g_ref, wq_ref, km_ref, xn_ref, e_ref, gate_ref,
                       pq_ref, s_ref, topv_ref, topi_ref, cv_ref, ce_ref, sc_ref, rese_ref, resg_ref):
    k_top = PEER_TOPK
    xn = _rms(h_ref[...], g_ref[...]).astype(BF16)
    xn_ref[...] = xn
    neg_inf = -jnp.inf
    none_yet = jnp.full((SUBLANES, LANES), -1, jnp.int32)
    pq_all = _dot(xn, wq_ref[...]).astype(BF16)
    for hd in range(PEER_HEADS):
        pq_ref[hd] = pq_all[:, hd * LANES:(hd + 1) * LANES]

    def head(hd, _):
        pq = pq_ref[hd]
        for c in range(SUBLANES):
            st = _dot_nt(km_ref[hd], pq[c * LANES:(c + 1) * LANES])
            s_ref[pl.ds(c, 2 * PEER_NKEYS, stride=SUBLANES), :] = st

        for p in range(2):
            def extract(k, prev, p=p):
                def leaf(n):
                    r = _vreg_rows(p * PEER_NKEYS + n)
                    v = jnp.where(prev == n, neg_inf, s_ref[r, :])
                    s_ref[r, :] = v
                    return v, n
                m, im = _argmax_tree(leaf, 0, PEER_NKEYS)
                topv_ref[_vreg_rows(p * k_top + k), :] = m
                topi_ref[_vreg_rows(p * k_top + k), :] = im
                return im
            lax.fori_loop(0, k_top, extract, none_yet)

        for idx, (i, j) in enumerate(_PEER_PAIRS):
            cv_ref[_vreg_rows(idx), :] = topv_ref[_vreg_rows(i), :] + topv_ref[_vreg_rows(k_top + j), :]
            ce_ref[_vreg_rows(idx), :] = (topi_ref[_vreg_rows(i), :] * PEER_NKEYS
                                          + topi_ref[_vreg_rows(k_top + j), :])

        def pick(k, prev):
            def leaf(idx):
                r = _vreg_rows(idx)
                eid = ce_ref[r, :]
                v = jnp.where(eid == prev, neg_inf, cv_ref[r, :])
                cv_ref[r, :] = v
                return v, eid
            m, em = _argmax_tree(leaf, 0, len(_PEER_PAIRS))
            sc_ref[_vreg_rows(k), :] = m
            rese_ref[_vreg_rows(hd * k_top + k), :] = em.astype(F32)
            return em
        lax.fori_loop(0, k_top, pick, none_yet)

        mx = sc_ref[_vreg_rows(0), :]
        exs = [jnp.exp(sc_ref[_vreg_rows(k), :] - mx) for k in range(k_top)]
        total = exs[0]
        for ex in exs[1:]:
            total = total + ex
        for k in range(k_top):
            resg_ref[_vreg_rows(hd * k_top + k), :] = exs[k] / total
        return 0

    lax.fori_loop(0, PEER_HEADS, head, 0)
    for c in range(SUBLANES):
        rows = slice(c * LANES, (c + 1) * LANES)
        e_ref[rows, :] = jnp.transpose(rese_ref[pl.ds(c, LANES, stride=SUBLANES), :]).astype(jnp.int32)
        gate_ref[rows, :] = jnp.transpose(resg_ref[pl.ds(c, LANES, stride=SUBLANES), :])


def _peer_route(h1, g, wq, km):
    n = h1.shape[0]
    tm = ROUTE_TILE
    row = lambda i: (i, 0)
    vregs = lambda count, dtype: pltpu.VMEM((count * SUBLANES, LANES), dtype)
    return pl.pallas_call(
        _peer_route_kernel,
        grid=(n // tm,),
        in_specs=[pl.BlockSpec((tm, D_MODEL), row), pl.BlockSpec((1, D_MODEL), lambda i: (0, 0)),
                  pl.BlockSpec((D_MODEL, PEER_HEADS * LANES), lambda i: (0, 0)),
                  pl.BlockSpec((PEER_HEADS, 2 * PEER_NKEYS, LANES), lambda i: (0, 0, 0))],
        out_specs=[pl.BlockSpec((tm, D_MODEL), row), pl.BlockSpec((tm, LANES), row),
                   pl.BlockSpec((tm, LANES), row)],
        out_shape=[jax.ShapeDtypeStruct((n, D_MODEL), BF16),
                   jax.ShapeDtypeStruct((n, LANES), jnp.int32),
                   jax.ShapeDtypeStruct((n, LANES), F32)],
        scratch_shapes=[pltpu.VMEM((PEER_HEADS, tm, LANES), BF16),
                        vregs(2 * PEER_NKEYS, F32), vregs(2 * PEER_TOPK, F32), vregs(2 * PEER_TOPK, jnp.int32),
                        vregs(len(_PEER_PAIRS), F32), vregs(len(_PEER_PAIRS), jnp.int32),
                        vregs(PEER_TOPK, F32), vregs(PEER_HEADS * PEER_TOPK, F32),
                        vregs(PEER_HEADS * PEER_TOPK, F32)],
        compiler_params=pltpu.CompilerParams(dimension_semantics=("parallel",),
                                             vmem_limit_bytes=VMEM_LIMIT),
        name="peer_route",
    )(h1, g, wq, km)


def _peer_key_matrix(sub_keys):
    z = jnp.zeros_like(sub_keys[:, 0])
    k0 = jnp.concatenate([sub_keys[:, 0], z], axis=2)
    k1 = jnp.concatenate([z, sub_keys[:, 1]], axis=2)
    return jnp.concatenate([k0, k1], axis=1).astype(BF16)


def _peer_dense_kernel(xn_ref, e_ref, g_ref, u_ref, v_ref, o_ref, gs_ref):
    tt = xn_ref.shape[0]
    ce = u_ref.shape[0]
    stride = tt + GS_PAD
    c = pl.program_id(1)

    @pl.when(c == 0)
    def _():
        o_ref[...] = jnp.zeros_like(o_ref)
        sub = lax.broadcasted_iota(jnp.int32, (PEER_NKEYS, LANES), 0)

        def tok(t, _):
            e_row = e_ref[pl.ds(t, 1), :]
            g_row = g_ref[pl.ds(t, 1), :]
            left = jnp.where(sub == e_row // PEER_NKEYS, g_row, 0.0).astype(BF16)
            right = jnp.where(sub == e_row % PEER_NKEYS, 1.0, 0.0).astype(BF16)
            gmap = _dot_nt(left, right).astype(BF16)
            gs_ref[pl.ds(t, PEER_NKEYS // 2, stride=stride), :] = pltpu.bitcast(gmap, jnp.uint32)
            return 0

        lax.fori_loop(0, tt, tok, 0, unroll=BUILD_UNROLL)

    hid = _dot_nt(xn_ref[...], u_ref[...])
    slabs = []
    for j in range(ce // LANES // 2):
        start = pl.multiple_of((c * (ce // LANES // 2) + j) * stride, SUBLANES)
        packed = gs_ref[pl.ds(start, tt), :]
        slabs.append(lax.bitcast_convert_type(packed << 16, F32))
        slabs.append(lax.bitcast_convert_type(packed & jnp.uint32(0xFFFF0000), F32))
    gate = jnp.concatenate(slabs, axis=1)
    a = (gate * _gelu(hid)).astype(BF16)
    o_ref[...] += _dot(a, v_ref[...])


def _peer_dense(xn, e, g, u, v, layer):
    n = xn.shape[0]
    tt = PEER_TILE
    ce = PEER_CHUNK
    row = lambda i, c: (i, 0)
    chunk = lambda i, c: (layer, c, 0)
    return pl.pallas_call(
        _peer_dense_kernel,
        grid=(n // tt, PEER_EXPERTS // ce),
        in_specs=[pl.BlockSpec((tt, D_MODEL), row), pl.BlockSpec((tt, LANES), row),
                  pl.BlockSpec((tt, LANES), row), pl.BlockSpec((None, ce, D_MODEL), chunk),
                  pl.BlockSpec((None, ce, D_MODEL), chunk)],
        out_specs=pl.BlockSpec((tt, D_MODEL), row),
        out_shape=jax.ShapeDtypeStruct((n, D_MODEL), F32),
        scratch_shapes=[pltpu.VMEM((PEER_NKEYS // 2 * (tt + GS_PAD), LANES), jnp.uint32)],
        compiler_params=pltpu.CompilerParams(dimension_semantics=("parallel", "arbitrary"),
                                             vmem_limit_bytes=PEER_VMEM_LIMIT),
        name="peer_dense",
    )(xn, e, g, u, v)


def _ple_kernel(h_ref, peer_ref, pp_ref, ps_ref, g_ref, wg_ref, bg_ref, wp_ref, gf_ref, op_ref, os_ref, *,
                n_prompt_tiles, final):
    is_prompt = pl.program_id(0) < n_prompt_tiles
    h = h_ref[...] + peer_ref[...]
    xn = _rms(h, g_ref[...]).astype(BF16)
    gate = jax.nn.sigmoid(_dot(xn, wg_ref[...]) + bg_ref[...])
    p = jnp.where(is_prompt, pp_ref[...], ps_ref[...])
    out = h + _dot(p.astype(BF16), wp_ref[...]) * gate
    if final:
        out = _rms(out, gf_ref[...])

    @pl.when(is_prompt)
    def _():
        op_ref[...] = out

    @pl.when(jnp.logical_not(is_prompt))
    def _():
        os_ref[...] = out


def _ple(h1, peer_out, p_prompt, p_sample, g, wg, bg, wp, gf, layer, final):
    n = h1.shape[0]
    n_prompt = p_prompt.shape[1]
    tm = TOK_TILE
    npt = n_prompt // tm
    row = lambda i: (i, 0)
    fixed = lambda i: (0, 0)
    pp_blk, ps_blk = _split_rows_specs(layer, tm, PLE_DIM, npt)
    return pl.pallas_call(
        functools.partial(_ple_kernel, n_prompt_tiles=npt, final=final),
        grid=(n // tm,),
        in_specs=[pl.BlockSpec((tm, D_MODEL), row), pl.BlockSpec((tm, D_MODEL), row), pp_blk, ps_blk,
                  pl.BlockSpec((1, D_MODEL), fixed), pl.BlockSpec((D_MODEL, D_MODEL), fixed),
                  pl.BlockSpec((1, D_MODEL), fixed), pl.BlockSpec((PLE_DIM, D_MODEL), fixed),
                  pl.BlockSpec((1, D_MODEL), fixed)],
        out_specs=list(_split_rows_2d(tm, D_MODEL, npt)),
        out_shape=[jax.ShapeDtypeStruct((n_prompt, D_MODEL), F32),
                   jax.ShapeDtypeStruct((n - n_prompt, D_MODEL), F32)],
        compiler_params=pltpu.CompilerParams(dimension_semantics=("arbitrary",),
                                             vmem_limit_bytes=VMEM_LIMIT),
        name="ple",
    )(h1, peer_out, p_prompt, p_sample, g, wg, bg, wp, gf)


def kernel(x_prompt, x_sample, cache_k, cache_v, p_prompt, p_sample, g_mix, w_in, g_sb_out, w_spatial, b_spatial, g_a_v, g_a_out, w_out, g_ffn, w_peer_q, peer_sub_keys, peer_u, peer_v, g_ple, w_ple_gate, b_ple_gate, w_ple, g_final):
    depth = w_in.shape[0]
    b, t, d = x_prompt.shape
    bs, ts, _ = x_sample.shape
    past = cache_k.shape[2]
    n_p, n_s = b * t, bs * ts
    assert d == D_MODEL and t % SB_BLOCK == 0 and past % SB_BLOCK == 0 and t % TOK_TILE == 0
    assert n_p % TOK_TILE == 0 and n_s % TOK_TILE == 0 and MIX_L % ts == 0 and n_p % ts == 0
    assert (n_p + n_s) % ROUTE_TILE == 0 and (n_p + n_s) % PEER_TILE == 0
    assert bs % SB_SAMPLE_STREAMS == 0 and n_p % (SB_SAMPLE_STREAMS * ts) == 0

    h_p, h_s = x_prompt.reshape(n_p, d), x_sample.reshape(n_s, d)
    row = lambda a: a.reshape(1, -1)
    pp, ps = p_prompt.reshape(depth, n_p, PLE_DIM), p_sample.reshape(depth, n_s, PLE_DIM)
    u_tab, v_tab = peer_u.astype(BF16), peer_v.astype(BF16)
    cache_kt = jnp.transpose(cache_k, (0, 1, 3, 4, 2)).reshape(depth, bs, SB_W, past)
    cache_vt = jnp.transpose(cache_v, (0, 1, 3, 4, 2)).reshape(depth, bs, SB_W, past)
    kv = None
    vas = None
    for l in range(depth):
        wkv_t = jnp.transpose(w_in[l][:, SB_W:3 * SB_W]).astype(BF16)
        (q, ktb, vtb, ua, va), kv = _inproj(h_p, h_s, row(g_mix[l]), w_in[l].astype(BF16), wkv_t, l, depth, b, t, kv)
        o = _sb_prompt(q, ktb, vtb, b, t)
        o = _sb_sample(q, kv[2], kv[3], cache_kt, cache_vt, o, l, n_p, ts)
        ws, bsp = _spatial_params(w_spatial[l], b_spatial[l], ts)
        h1, vas = _mixer_out(h_p, h_s, ua, va, o, ws, bsp, row(g_a_v[l]), row(g_a_out[l]), row(g_sb_out[l]),
                             w_out[l].astype(BF16), l, depth, vas)
        xn, e, gate = _peer_route(h1, row(g_ffn[l]), w_peer_q[l].astype(BF16), _peer_key_matrix(peer_sub_keys[l]))
        peer_out = _peer_dense(xn, e, gate, u_tab, v_tab, l)
        h_p, h_s = _ple(h1, peer_out, pp, ps, row(g_ple[l]), w_ple_gate[l].astype(BF16), row(b_ple_gate[l]),
                        w_ple[l].astype(BF16), row(g_final), l, final=(l == depth - 1))
    y_p, y_s = h_p, h_s
    k_p, v_p, k_s, v_s = kv

    def untranspose(a):
        return jnp.transpose(a.reshape(depth, b, N_HEADS, HEAD_DIM, t), (0, 1, 4, 2, 3))

    return (y_p.reshape(b, t, d), y_s.reshape(bs, ts, d), untranspose(k_p), untranspose(v_p),
            k_s.reshape(depth, bs, ts, N_HEADS, HEAD_DIM), v_s.reshape(depth, bs, ts, N_HEADS, HEAD_DIM),
            vas.reshape(depth, bs, ts, A_W // GROUP, GROUP))
```

```python
import functools

import jax
import jax.numpy as jnp
from jax import lax
from jax.experimental import pallas as pl
from jax.experimental.pallas import tpu as pltpu

F32 = jnp.float32
BF16 = jnp.bfloat16

LANES = 128
SUBLANES = 8
D_MODEL = 1024
N_HEADS = 8
HEAD_DIM = 64
SB_W = N_HEADS * HEAD_DIM
A_W = 512
IN_W = 3 * SB_W + 2 * A_W
GROUP = 64
MIX_L = 128
PEER_HEADS = 8
PEER_NKEYS = 128
PEER_TOPK = 16
PEER_EXPERTS = PEER_NKEYS * PEER_NKEYS
PLE_DIM = 256
RMS_EPS = 1e-6
TOK_TILE = 512
ROUTE_TILE = SUBLANES * LANES
SB_BLOCK = 256
SB_SAMPLE_STREAMS = 2
SB_STEP_HEADS = 8
PEER_TILE = 512
PEER_CHUNK = 2048
PEER_VMEM_LIMIT = 58 * 1024 * 1024
BUILD_UNROLL = 128
GS_PAD = 8
VMEM_LIMIT = 48 * 1024 * 1024
INV_SQRT2 = 0.7071067811865476


def _dot(a, b):
    return jnp.dot(a, b, preferred_element_type=F32)


def _dot_nt(a, b):
    return lax.dot_general(a, b, (((1,), (1,)), ((), ())), preferred_element_type=F32)


def _split_bf16(x):
    hi = x.astype(BF16)
    lo = (x - hi.astype(F32)).astype(BF16)
    return hi, lo


def _gelu(x):
    return 0.5 * x * (1.0 + lax.erf(x * INV_SQRT2))


def _rms(x, g):
    ms = jnp.mean(x * x, axis=-1, keepdims=True)
    return x * lax.rsqrt(ms + RMS_EPS) * g


def _group_mean_matrix():
    r = lax.broadcasted_iota(jnp.int32, (LANES, LANES), 0) // GROUP
    c = lax.broadcasted_iota(jnp.int32, (LANES, LANES), 1) // GROUP
    return jnp.where(r == c, 1.0 / GROUP, 0.0).astype(BF16)


def _group_rms(x, g, bd):
    xx = x * x
    hi, lo = _split_bf16(xx)
    parts = []
    for j in range(x.shape[1] // LANES):
        sl = slice(LANES * j, LANES * (j + 1))
        parts.append(_dot(hi[:, sl], bd) + _dot(lo[:, sl], bd))
    ms = jnp.concatenate(parts, axis=1)
    return x * lax.rsqrt(ms + RMS_EPS) * g


def _inproj_kernel(xp_ref, xs_ref, g_ref, w_ref, wkv_t_ref, *refs, n_prompt_tiles):
    q_ref, ktb_ref, vtb_ref, ua_ref, va_ref, kp_ref, vp_ref, ks_ref, vs_ref = refs[-9:]
    is_prompt = pl.program_id(0) < n_prompt_tiles
    xn = _rms(jnp.where(is_prompt, xp_ref[...], xs_ref[...]), g_ref[...]).astype(BF16)
    q = _dot(xn, w_ref[:, 0:SB_W])
    q_ref[...] = (q * (HEAD_DIM ** -0.5)).astype(BF16)
    ua_ref[...] = _dot(xn, w_ref[:, 3 * SB_W:3 * SB_W + A_W])
    va_ref[...] = _dot(xn, w_ref[:, 3 * SB_W + A_W:IN_W])
    kvt = _dot_nt(wkv_t_ref[...], xn)
    kt, vt = kvt[0:SB_W], kvt[SB_W:2 * SB_W]
    for j in range(ktb_ref.shape[0]):
        ktb_ref[j] = kt[:, j * SB_BLOCK:(j + 1) * SB_BLOCK].astype(BF16)
        vtb_ref[j] = vt[:, j * SB_BLOCK:(j + 1) * SB_BLOCK].astype(BF16)

    @pl.when(is_prompt)
    def _():
        kp_ref[...] = kt
        vp_ref[...] = vt

    @pl.when(jnp.logical_not(is_prompt))
    def _():
        ks_ref[...] = _dot(xn, w_ref[:, SB_W:2 * SB_W])
        vs_ref[...] = _dot(xn, w_ref[:, 2 * SB_W:3 * SB_W])


def _split_rows_specs(layer, tile, width, n_prompt_tiles):
    prompt = pl.BlockSpec((None, tile, width), lambda i: (layer, jnp.minimum(i, n_prompt_tiles - 1), 0))
    sample = pl.BlockSpec((None, tile, width), lambda i: (layer, jnp.maximum(i - n_prompt_tiles, 0), 0))
    return prompt, sample


def _split_rows_2d(tile, width, n_prompt_tiles):
    return (pl.BlockSpec((tile, width), lambda i: (jnp.minimum(i, n_prompt_tiles - 1), 0)),
            pl.BlockSpec((tile, width), lambda i: (jnp.maximum(i - n_prompt_tiles, 0), 0)))


def _inproj(h_p, h_s, g, w, wkv_t, layer, depth, batch, seq, stacked):
    n = h_p.shape[0] + h_s.shape[0]
    tm = TOK_TILE
    n_prompt = batch * seq
    npt, per_seq = n_prompt // tm, seq // tm
    row = lambda i: (i, 0)
    fixed = lambda i: (0, 0)
    out_f = jax.ShapeDtypeStruct((n, SB_W), F32)
    out_b = jax.ShapeDtypeStruct((n, SB_W), BF16)
    out_t = jax.ShapeDtypeStruct((n // SB_BLOCK, SB_W, SB_BLOCK), BF16)
    stack_p = jax.ShapeDtypeStruct((depth, batch, SB_W, seq), F32)
    stack_s = jax.ShapeDtypeStruct((depth, n - n_prompt, SB_W), F32)
    blk = pl.BlockSpec((tm, SB_W), row)
    blk_t = pl.BlockSpec((tm // SB_BLOCK, SB_W, SB_BLOCK), lambda i: (i, 0, 0))
    _, sample_blk = _split_rows_specs(layer, tm, SB_W, npt)

    def prompt_idx(i):
        t = jnp.minimum(i, npt - 1)
        return (layer, t // per_seq, 0, t % per_seq)

    prompt_blk = pl.BlockSpec((None, None, SB_W, tm), prompt_idx)
    carried = [] if stacked is None else list(stacked)
    outs = pl.pallas_call(
        functools.partial(_inproj_kernel, n_prompt_tiles=npt),
        grid=(n // tm,),
        in_specs=list(_split_rows_2d(tm, D_MODEL, npt))
                 + [pl.BlockSpec((1, D_MODEL), fixed), pl.BlockSpec((D_MODEL, IN_W), fixed),
                    pl.BlockSpec((2 * SB_W, D_MODEL), fixed)]
                 + [pl.BlockSpec(memory_space=pl.ANY)] * len(carried),
        out_specs=[blk, blk_t, blk_t, blk, blk, prompt_blk, prompt_blk, sample_blk, sample_blk],
        out_shape=[out_b, out_t, out_t, out_f, out_f, stack_p, stack_p, stack_s, stack_s],
        input_output_aliases={5 + j: 5 + j for j in range(len(carried))},
        compiler_params=pltpu.CompilerParams(dimension_semantics=("arbitrary",),
                                             vmem_limit_bytes=VMEM_LIMIT),
        name="inproj",
    )(h_p, h_s, g, w, wkv_t, *carried)
    return outs[:5], tuple(outs[5:])


def _strict_lower_ones(n):
    r = lax.broadcasted_iota(jnp.int32, (n, n), 0)
    c = lax.broadcasted_iota(jnp.int32, (n, n), 1)
    return jnp.where(r > c, 1.0, 0.0).astype(BF16)


def _sb_blocks(chains, ustrict, mask):
    zs = [_dot(qh, kt) for qh, kt, _, _ in chains]
    log_betas, log_1ms, splits = [], [], []
    for z in zs:
        sp = jnp.log(1.0 + jnp.exp(-jnp.abs(z)))
        log_beta = jnp.minimum(z, 0.0) - sp
        log_1m = log_beta - z
        if mask is not None:
            log_1m = jnp.where(mask, log_1m, 0.0)
        log_betas.append(log_beta)
        log_1ms.append(log_1m)
        splits.append(_split_bf16(log_1m))
    laters = [_dot(hi, ustrict) + _dot(lo, ustrict) for hi, lo in splits]
    ws, carries = [], []
    for (_, kt, _, carry), log_beta, log_1m, later in zip(chains, log_betas, log_1ms, laters):
        rest = later + jnp.concatenate([carry] * (kt.shape[1] // LANES), axis=1)
        w = jnp.exp(log_beta + rest)
        if mask is not None:
            w = jnp.where(mask, w, 0.0)
        ws.append(w.astype(BF16))
        carries.append(carry + jnp.broadcast_to(later[:, :1] + log_1m[:, :1], carry.shape))
    return [(_dot_nt(w, vth), carry) for w, (_, _, vth, _), carry in zip(ws, chains, carries)]


def _sb_prompt_kernel(q_ref, kt_ref, vt_ref, o_ref, acc_ref, carry_ref):
    tq = q_ref.shape[0]
    tk = tq
    qi = pl.program_id(2)
    ustrict = _strict_lower_ones(tk)
    r = lax.broadcasted_iota(jnp.int32, (tq, tk), 0)
    c = lax.broadcasted_iota(jnp.int32, (tq, tk), 1)
    diag = c < r
    lane_head = lax.broadcasted_iota(jnp.int32, (1, LANES), 1) // HEAD_DIM
    row_head = lax.broadcasted_iota(jnp.int32, (LANES, 1), 0) // HEAD_DIM
    n_pairs = q_ref.shape[1] // LANES
    lanes = [slice(pr * LANES, (pr + 1) * LANES) for pr in range(n_pairs)]
    qhs = [[jnp.where(lane_head == hh, q_ref[:, ln], jnp.zeros((tq, LANES), BF16)) for hh in range(2)]
           for ln in lanes]
    acc_ref[...] = jnp.zeros_like(acc_ref)
    carry_ref[...] = jnp.zeros_like(carry_ref)

    def step(kb, mask):
        chains = []
        for pr, ln in enumerate(lanes):
            kt = kt_ref[kb, ln, :]
            vt = vt_ref[kb, ln, :]
            for hh in range(2):
                vth = jnp.where(row_head == hh, vt, jnp.zeros_like(vt))
                chains.append((qhs[pr][hh], kt, vth, carry_ref[2 * pr + hh]))
        results = _sb_blocks(chains, ustrict, mask)
        for i, (_, carry) in enumerate(results):
            carry_ref[i] = carry
        for pr, ln in enumerate(lanes):
            acc_ref[:, ln] += results[2 * pr][0] + results[2 * pr + 1][0]

    step(qi, diag)

    def body(it, _):
        step(qi - 1 - it, None)
        return 0

    lax.fori_loop(0, qi, body, 0)
    o_ref[...] = acc_ref[...]


def _sb_prompt(q, ktb, vtb, batch, seq):
    tq = SB_BLOCK
    nq = seq // tq
    width = SB_STEP_HEADS * HEAD_DIM
    return pl.pallas_call(
        _sb_prompt_kernel,
        grid=(batch, SB_W // width, nq),
        in_specs=[pl.BlockSpec((tq, width), lambda b, j, i: (b * nq + i, j)),
                  pl.BlockSpec((nq, width, tq), lambda b, j, i: (b, j, 0)),
                  pl.BlockSpec((nq, width, tq), lambda b, j, i: (b, j, 0))],
        out_specs=pl.BlockSpec((tq, width), lambda b, j, i: (b * nq + i, j)),
        out_shape=jax.ShapeDtypeStruct((q.shape[0], SB_W), F32),
        scratch_shapes=[pltpu.VMEM((tq, width), F32), pltpu.VMEM((SB_STEP_HEADS, tq, LANES), F32)],
        compiler_params=pltpu.CompilerParams(
            dimension_semantics=("parallel", "parallel", "arbitrary"), vmem_limit_bytes=VMEM_LIMIT),
        name="sb_prompt",
    )(q, ktb, vtb)


def _sb_sample_kernel(q_ref, k_ref, v_ref, ck_ref, cv_ref, prompt_rows_ref, o_ref, *, ts):
    del prompt_rows_ref
    streams = q_ref.shape[0] // ts
    past = ck_ref.shape[2]
    tk = SB_BLOCK
    ustrict = _strict_lower_ones(tk)
    ustrict_d = ustrict[:LANES, :LANES]
    r = lax.broadcasted_iota(jnp.int32, (ts, LANES), 0)
    c = lax.broadcasted_iota(jnp.int32, (ts, LANES), 1)
    diag = c < r
    lane_head = lax.broadcasted_iota(jnp.int32, (1, LANES), 1) // HEAD_DIM
    row_head = lax.broadcasted_iota(jnp.int32, (LANES, 1), 0) // HEAD_DIM
    pad = jnp.zeros((LANES - ts, LANES), F32)
    n_pairs = q_ref.shape[1] // LANES
    lanes = [slice(pr * LANES, (pr + 1) * LANES) for pr in range(n_pairs)]
    rows = [slice(s * ts, (s + 1) * ts) for s in range(streams)]
    heads = [(s, pr, hh, jnp.where(lane_head == hh, q_ref[rows[s], lanes[pr]], jnp.zeros((ts, LANES), BF16)))
             for s in range(streams) for pr in range(n_pairs) for hh in range(2)]

    def own(hh, vt):
        return jnp.where(row_head == hh, vt, jnp.zeros_like(vt))

    def new_t(ref, s, pr):
        return jnp.transpose(jnp.concatenate([ref[rows[s], lanes[pr]], pad], axis=0)).astype(BF16)

    zero = jnp.zeros((ts, LANES), F32)
    new_k = [[new_t(k_ref, s, pr) for pr in range(n_pairs)] for s in range(streams)]
    new_v = [[new_t(v_ref, s, pr) for pr in range(n_pairs)] for s in range(streams)]
    results = _sb_blocks([(qh, new_k[s][pr], own(hh, new_v[s][pr]), zero) for s, pr, hh, qh in heads],
                         ustrict_d, diag)
    accs = [out for out, _ in results]
    for kb in range(past // tk - 1, -1, -1):
        keys = slice(kb * tk, (kb + 1) * tk)
        old_k = [[ck_ref[s, lanes[pr], keys].astype(BF16) for pr in range(n_pairs)] for s in range(streams)]
        old_v = [[cv_ref[s, lanes[pr], keys].astype(BF16) for pr in range(n_pairs)] for s in range(streams)]
        results = _sb_blocks([(qh, old_k[s][pr], own(hh, old_v[s][pr]), carry)
                              for (s, pr, hh, qh), (_, carry) in zip(heads, results)], ustrict, None)
        accs = [acc + out for acc, (out, _) in zip(accs, results)]
    for s in range(streams):
        for pr in range(n_pairs):
            i = 2 * (s * n_pairs + pr)
            o_ref[rows[s], lanes[pr]] = accs[i] + accs[i + 1]


def _sb_sample(q, k_s, v_s, cache_kt, cache_vt, o, layer, n_prompt, ts):
    _, batch, width, past = cache_kt.shape
    per = SB_SAMPLE_STREAMS
    off = n_prompt // (per * ts)
    rows = pl.BlockSpec((per * ts, SB_W), lambda b: (off + b, 0))
    new = pl.BlockSpec((None, per * ts, SB_W), lambda b: (layer, b, 0))
    old = pl.BlockSpec((None, per, width, past), lambda b: (layer, b, 0, 0))
    return pl.pallas_call(
        functools.partial(_sb_sample_kernel, ts=ts),
        grid=(batch // per,),
        in_specs=[rows, new, new, old, old, pl.BlockSpec(memory_space=pl.ANY)],
        out_specs=rows,
        out_shape=jax.ShapeDtypeStruct(o.shape, o.dtype),
        input_output_aliases={5: 0},
        compiler_params=pltpu.CompilerParams(dimension_semantics=("parallel",),
                                             vmem_limit_bytes=VMEM_LIMIT),
        name="sb_sample",
    )(q, k_s, v_s, cache_kt, cache_vt, o)


def _mixer_out_kernel(hp_ref, hs_ref, ua_ref, va_ref, o_ref, ws_ref, bs_ref, gav_ref, gao_ref, gsb_ref, wo_ref,
                      *refs, n_prompt_tiles):
    h1_ref, vas_ref = refs[-2:]
    tm = ua_ref.shape[0]
    is_prompt = pl.program_id(0) < n_prompt_tiles
    bd = _group_mean_matrix()
    lane_group = lax.broadcasted_iota(jnp.int32, (1, LANES), 1) // GROUP
    u = _gelu(ua_ref[...])
    vn = _group_rms(_gelu(va_ref[...]), gav_ref[...], bd)

    @pl.when(jnp.logical_not(is_prompt))
    def _():
        vas_ref[...] = vn

    vnb = vn.astype(BF16)
    rows = []
    for rr in range(tm // MIX_L):
        slabs = []
        for j in range(A_W // LANES):
            v2 = vnb[rr * MIX_L:(rr + 1) * MIX_L, j * LANES:(j + 1) * LANES]
            mix = jnp.zeros((MIX_L, LANES), F32)
            for gg in range(2):
                v2g = jnp.where(lane_group == gg, v2, jnp.zeros_like(v2))
                mix = mix + _dot(ws_ref[2 * j + gg], v2g)
            slabs.append(mix)
        rows.append(jnp.concatenate(slabs, axis=1) + bs_ref[...])
    mix = jnp.concatenate(rows, axis=0)
    oa = _group_rms(u * mix, gao_ref[...], bd)
    osb = _group_rms(o_ref[...], gsb_ref[...], bd)
    cat = jnp.concatenate([osb, oa], axis=1).astype(BF16)
    h1_ref[...] = jnp.where(is_prompt, hp_ref[...], hs_ref[...]) + _dot(cat, wo_ref[...])


def _mixer_out(h_p, h_s, ua, va, o, ws, bs, gav, gao, gsb, wo, layer, depth, stacked):
    n_prompt = h_p.shape[0]
    n = n_prompt + h_s.shape[0]
    tm = TOK_TILE
    npt = n_prompt // tm
    row = lambda i: (i, 0)
    fixed = lambda i: (0, 0)
    variant = lambda i: jnp.where(i < npt, 0, 1)
    _, sample_blk = _split_rows_specs(layer, tm, A_W, npt)
    carried = [] if stacked is None else [stacked]
    return pl.pallas_call(
        functools.partial(_mixer_out_kernel, n_prompt_tiles=npt),
        grid=(n // tm,),
        in_specs=list(_split_rows_2d(tm, D_MODEL, npt))
                 + [pl.BlockSpec((tm, A_W), row), pl.BlockSpec((tm, A_W), row), pl.BlockSpec((tm, SB_W), row),
                    pl.BlockSpec((None, 2 * A_W // LANES, MIX_L, MIX_L), lambda i: (variant(i), 0, 0, 0)),
                    pl.BlockSpec((None, MIX_L, A_W), lambda i: (variant(i), 0, 0)),
                    pl.BlockSpec((1, A_W), fixed), pl.BlockSpec((1, A_W), fixed),
                    pl.BlockSpec((1, SB_W), fixed), pl.BlockSpec((D_MODEL, D_MODEL), fixed)]
                 + [pl.BlockSpec(memory_space=pl.ANY)] * len(carried),
        out_specs=[pl.BlockSpec((tm, D_MODEL), row), sample_blk],
        out_shape=[jax.ShapeDtypeStruct((n, D_MODEL), F32),
                   jax.ShapeDtypeStruct((depth, n - n_prompt, A_W), F32)],
        input_output_aliases={11: 1} if carried else {},
        compiler_params=pltpu.CompilerParams(dimension_semantics=("arbitrary",),
                                             vmem_limit_bytes=VMEM_LIMIT),
        name="mixer_out",
    )(h_p, h_s, ua, va, o, ws, bs, gav, gao, gsb, wo, *carried)


def _spatial_params(w_s, b_s, ts):
    tril = jnp.tril(jnp.ones((MIX_L, MIX_L), bool))
    w_p = jnp.where(tril[None], w_s, 0.0)
    rep = MIX_L // ts
    w_small = jnp.where(tril[None, :ts, :ts], w_s[:, :ts, :ts], 0.0)
    eye = jnp.eye(rep, dtype=w_s.dtype)
    w_smp = jnp.einsum("ab,gts->gatbs", eye, w_small).reshape(-1, MIX_L, MIX_L)
    b_p = jnp.repeat(jnp.transpose(b_s), A_W // b_s.shape[0], axis=1)
    b_smp = jnp.tile(b_p[:ts], (rep, 1))
    return jnp.stack([w_p, w_smp]).astype(BF16), jnp.stack([b_p, b_smp])


def _argmax_tree(leaf, lo, hi):
    if hi - lo == 1:
        return leaf(lo)
    mid = (lo + hi) // 2
    va, ta = _argmax_tree(leaf, lo, mid)
    vb, tb = _argmax_tree(leaf, mid, hi)
    return jnp.maximum(va, vb), jnp.where(va >= vb, ta, tb)


def _vreg_rows(idx):
    if isinstance(idx, int):
        return slice(idx * SUBLANES, (idx + 1) * SUBLANES)
    return pl.ds(pl.multiple_of(idx * SUBLANES, SUBLANES), SUBLANES)


_PEER_PAIRS = [(i, j) for i in range(PEER_TOPK) for j in range(PEER_TOPK) if (i + 1) * (j + 1) <= PEER_TOPK]


def _peer_route_kernel(h_ref, g_ref, wq_ref, km_ref, xn_ref, e_ref, gate_ref,
                       pq_ref, s_ref, topv_ref, topi_ref, cv_ref, ce_ref, sc_ref, rese_ref, resg_ref):
    k_top = PEER_TOPK
    xn = _rms(h_ref[...], g_ref[...]).astype(BF16)
    xn_ref[...] = xn
    neg_inf = -jnp.inf
    none_yet = jnp.full((SUBLANES, LANES), -1, jnp.int32)
    pq_all = _dot(xn, wq_ref[...]).astype(BF16)
    for hd in range(PEER_HEADS):
        pq_ref[hd] = pq_all[:, hd * LANES:(hd + 1) * LANES]

    def head(hd, _):
        pq = pq_ref[hd]
        for c in range(SUBLANES):
            st = _dot_nt(km_ref[hd], pq[c * LANES:(c + 1) * LANES])
            s_ref[pl.ds(c, 2 * PEER_NKEYS, stride=SUBLANES), :] = st

        for p in range(2):
            def extract(k, prev, p=p):
                def leaf(n):
                    r = _vreg_rows(p * PEER_NKEYS + n)
                    v = jnp.where(prev == n, neg_inf, s_ref[r, :])
                    s_ref[r, :] = v
                    return v, n
                m, im = _argmax_tree(leaf, 0, PEER_NKEYS)
                topv_ref[_vreg_rows(p * k_top + k), :] = m
                topi_ref[_vreg_rows(p * k_top + k), :] = im
                return im
            lax.fori_loop(0, k_top, extract, none_yet)

        for idx, (i, j) in enumerate(_PEER_PAIRS):
            cv_ref[_vreg_rows(idx), :] = topv_ref[_vreg_rows(i), :] + topv_ref[_vreg_rows(k_top + j), :]
            ce_ref[_vreg_rows(idx), :] = (topi_ref[_vreg_rows(i), :] * PEER_NKEYS
                                          + topi_ref[_vreg_rows(k_top + j), :])

        def pick(k, prev):
            def leaf(idx):
                r = _vreg_rows(idx)
                eid = ce_ref[r, :]
                v = jnp.where(eid == prev, neg_inf, cv_ref[r, :])
                cv_ref[r, :] = v
                return v, eid
            m, em = _argmax_tree(leaf, 0, len(_PEER_PAIRS))
            sc_ref[_vreg_rows(k), :] = m
            rese_ref[_vreg_rows(hd * k_top + k), :] = em.astype(F32)
            return em
        lax.fori_loop(0, k_top, pick, none_yet)

        mx = sc_ref[_vreg_rows(0), :]
        exs = [jnp.exp(sc_ref[_vreg_rows(k), :] - mx) for k in range(k_top)]
        total = exs[0]
        for ex in exs[1:]:
            total = total + ex
        for k in range(k_top):
            resg_ref[_vreg_rows(hd * k_top + k), :] = exs[k] / total
        return 0

    lax.fori_loop(0, PEER_HEADS, head, 0)
    for c in range(SUBLANES):
        rows = slice(c * LANES, (c + 1) * LANES)
        e_ref[rows, :] = jnp.transpose(rese_ref[pl.ds(c, LANES, stride=SUBLANES), :]).astype(jnp.int32)
        gate_ref[rows, :] = jnp.transpose(resg_ref[pl.ds(c, LANES, stride=SUBLANES), :])


def _peer_route(h1, g, wq, km):
    n = h1.shape[0]
    tm = ROUTE_TILE
    row = lambda i: (i, 0)
    vregs = lambda count, dtype: pltpu.VMEM((count * SUBLANES, LANES), dtype)
    return pl.pallas_call(
        _peer_route_kernel,
        grid=(n // tm,),
        in_specs=[pl.BlockSpec((tm, D_MODEL), row), pl.BlockSpec((1, D_MODEL), lambda i: (0, 0)),
                  pl.BlockSpec((D_MODEL, PEER_HEADS * LANES), lambda i: (0, 0)),
                  pl.BlockSpec((PEER_HEADS, 2 * PEER_NKEYS, LANES), lambda i: (0, 0, 0))],
        out_specs=[pl.BlockSpec((tm, D_MODEL), row), pl.BlockSpec((tm, LANES), row),
                   pl.BlockSpec((tm, LANES), row)],
        out_shape=[jax.ShapeDtypeStruct((n, D_MODEL), BF16),
                   jax.ShapeDtypeStruct((n, LANES), jnp.int32),
                   jax.ShapeDtypeStruct((n, LANES), F32)],
        scratch_shapes=[pltpu.VMEM((PEER_HEADS, tm, LANES), BF16),
                        vregs(2 * PEER_NKEYS, F32), vregs(2 * PEER_TOPK, F32), vregs(2 * PEER_TOPK, jnp.int32),
                        vregs(len(_PEER_PAIRS), F32), vregs(len(_PEER_PAIRS), jnp.int32),
                        vregs(PEER_TOPK, F32), vregs(PEER_HEADS * PEER_TOPK, F32),
                        vregs(PEER_HEADS * PEER_TOPK, F32)],
        compiler_params=pltpu.CompilerParams(dimension_semantics=("parallel",),
                                             vmem_limit_bytes=VMEM_LIMIT),
        name="peer_route",
    )(h1, g, wq, km)


def _peer_key_matrix(sub_keys):
    z = jnp.zeros_like(sub_keys[:, 0])
    k0 = jnp.concatenate([sub_keys[:, 0], z], axis=2)
    k1 = jnp.concatenate([z, sub_keys[:, 1]], axis=2)
    return jnp.concatenate([k0, k1], axis=1).astype(BF16)


def _peer_dense_kernel(xn_ref, e_ref, g_ref, u_ref, v_ref, o_ref, gs_ref):
    tt = xn_ref.shape[0]
    ce = v_ref.shape[0]
    stride = tt + GS_PAD
    c = pl.program_id(1)

    @pl.when(c == 0)
    def _():
        o_ref[...] = jnp.zeros_like(o_ref)
        sub = lax.broadcasted_iota(jnp.int32, (PEER_NKEYS, LANES), 0)

        def tok(t, _):
            e_row = e_ref[pl.ds(t, 1), :]
            g_row = g_ref[pl.ds(t, 1), :]
            left = jnp.where(sub == e_row // PEER_NKEYS, g_row, 0.0).astype(BF16)
            right = jnp.where(sub == e_row % PEER_NKEYS, 1.0, 0.0).astype(BF16)
            gmap = _dot_nt(left, right).astype(BF16)
            gs_ref[pl.ds(t, PEER_NKEYS // 2, stride=stride), :] = pltpu.bitcast(gmap, jnp.uint32)
            return 0

        lax.fori_loop(0, tt, tok, 0, unroll=BUILD_UNROLL)

    hid = _dot(xn_ref[...], u_ref[...])
    slabs = []
    for j in range(ce // LANES // 2):
        start = pl.multiple_of((c * (ce // LANES // 2) + j) * stride, SUBLANES)
        packed = gs_ref[pl.ds(start, tt), :]
        slabs.append(lax.bitcast_convert_type(packed << 16, F32))
        slabs.append(lax.bitcast_convert_type(packed & jnp.uint32(0xFFFF0000), F32))
    gate = jnp.concatenate(slabs, axis=1)
    a = (gate * _gelu(hid)).astype(BF16)
    o_ref[...] += _dot(a, v_ref[...])


def _peer_dense(xn, e, g, u, v, layer):
    n = xn.shape[0]
    tt = PEER_TILE
    ce = PEER_CHUNK
    row = lambda i, c: (i, 0)
    chunk = lambda i, c: (layer, c, 0)
    return pl.pallas_call(
        _peer_dense_kernel,
        grid=(n // tt, PEER_EXPERTS // ce),
        in_specs=[pl.BlockSpec((tt, D_MODEL), row), pl.BlockSpec((tt, LANES), row),
                  pl.BlockSpec((tt, LANES), row), pl.BlockSpec((None, D_MODEL, ce), lambda i, c: (layer, 0, c)),
                  pl.BlockSpec((None, ce, D_MODEL), chunk)],
        out_specs=pl.BlockSpec((tt, D_MODEL), row),
        out_shape=jax.ShapeDtypeStruct((n, D_MODEL), F32),
        scratch_shapes=[pltpu.VMEM((PEER_NKEYS // 2 * (tt + GS_PAD), LANES), jnp.uint32)],
        compiler_params=pltpu.CompilerParams(dimension_semantics=("parallel", "arbitrary"),
                                             vmem_limit_bytes=PEER_VMEM_LIMIT),
        name="peer_dense",
    )(xn, e, g, u, v)


def _ple_kernel(h_ref, peer_ref, pp_ref, ps_ref, g_ref, wg_ref, bg_ref, wp_ref, gf_ref, op_ref, os_ref, *,
                n_prompt_tiles, final):
    is_prompt = pl.program_id(0) < n_prompt_tiles
    h = h_ref[...] + peer_ref[...]
    xn = _rms(h, g_ref[...]).astype(BF16)
    gate = jax.nn.sigmoid(_dot(xn, wg_ref[...]) + bg_ref[...])
    p = jnp.where(is_prompt, pp_ref[...], ps_ref[...])
    out = h + _dot(p.astype(BF16), wp_ref[...]) * gate
    if final:
        out = _rms(out, gf_ref[...])

    @pl.when(is_prompt)
    def _():
        op_ref[...] = out

    @pl.when(jnp.logical_not(is_prompt))
    def _():
        os_ref[...] = out


def _ple(h1, peer_out, p_prompt, p_sample, g, wg, bg, wp, gf, layer, final):
    n = h1.shape[0]
    n_prompt = p_prompt.shape[1]
    tm = TOK_TILE
    npt = n_prompt // tm
    row = lambda i: (i, 0)
    fixed = lambda i: (0, 0)
    pp_blk, ps_blk = _split_rows_specs(layer, tm, PLE_DIM, npt)
    return pl.pallas_call(
        functools.partial(_ple_kernel, n_prompt_tiles=npt, final=final),
        grid=(n // tm,),
        in_specs=[pl.BlockSpec((tm, D_MODEL), row), pl.BlockSpec((tm, D_MODEL), row), pp_blk, ps_blk,
                  pl.BlockSpec((1, D_MODEL), fixed), pl.BlockSpec((D_MODEL, D_MODEL), fixed),
                  pl.BlockSpec((1, D_MODEL), fixed), pl.BlockSpec((PLE_DIM, D_MODEL), fixed),
                  pl.BlockSpec((1, D_MODEL), fixed)],
        out_specs=list(_split_rows_2d(tm, D_MODEL, npt)),
        out_shape=[jax.ShapeDtypeStruct((n_prompt, D_MODEL), F32),
                   jax.ShapeDtypeStruct((n - n_prompt, D_MODEL), F32)],
        compiler_params=pltpu.CompilerParams(dimension_semantics=("arbitrary",),
                                             vmem_limit_bytes=VMEM_LIMIT),
        name="ple",
    )(h1, peer_out, p_prompt, p_sample, g, wg, bg, wp, gf)


def kernel(x_prompt, x_sample, cache_k, cache_v, p_prompt, p_sample, g_mix, w_in, g_sb_out, w_spatial, b_spatial, g_a_v, g_a_out, w_out, g_ffn, w_peer_q, peer_sub_keys, peer_u, peer_v, g_ple, w_ple_gate, b_ple_gate, w_ple, g_final):
    depth = w_in.shape[0]
    b, t, d = x_prompt.shape
    bs, ts, _ = x_sample.shape
    past = cache_k.shape[2]
    n_p, n_s = b * t, bs * ts
    assert d == D_MODEL and t % SB_BLOCK == 0 and past % SB_BLOCK == 0 and t % TOK_TILE == 0
    assert n_p % TOK_TILE == 0 and n_s % TOK_TILE == 0 and MIX_L % ts == 0 and n_p % ts == 0
    assert (n_p + n_s) % ROUTE_TILE == 0 and (n_p + n_s) % PEER_TILE == 0
    assert bs % SB_SAMPLE_STREAMS == 0 and n_p % (SB_SAMPLE_STREAMS * ts) == 0

    h_p, h_s = x_prompt.reshape(n_p, d), x_sample.reshape(n_s, d)
    row = lambda a: a.reshape(1, -1)
    pp, ps = p_prompt.reshape(depth, n_p, PLE_DIM), p_sample.reshape(depth, n_s, PLE_DIM)
    u_tab, v_tab = jnp.transpose(peer_u, (0, 2, 1)).astype(BF16), peer_v.astype(BF16)
    cache_kt = jnp.transpose(cache_k, (0, 1, 3, 4, 2)).reshape(depth, bs, SB_W, past)
    cache_vt = jnp.transpose(cache_v, (0, 1, 3, 4, 2)).reshape(depth, bs, SB_W, past)
    kv = None
    vas = None
    for l in range(depth):
        wkv_t = jnp.transpose(w_in[l][:, SB_W:3 * SB_W]).astype(BF16)
        (q, ktb, vtb, ua, va), kv = _inproj(h_p, h_s, row(g_mix[l]), w_in[l].astype(BF16), wkv_t, l, depth, b, t, kv)
        o = _sb_prompt(q, ktb, vtb, b, t)
        o = _sb_sample(q, kv[2], kv[3], cache_kt, cache_vt, o, l, n_p, ts)
        ws, bsp = _spatial_params(w_spatial[l], b_spatial[l], ts)
        h1, vas = _mixer_out(h_p, h_s, ua, va, o, ws, bsp, row(g_a_v[l]), row(g_a_out[l]), row(g_sb_out[l]),
                             w_out[l].astype(BF16), l, depth, vas)
        xn, e, gate = _peer_route(h1, row(g_ffn[l]), w_peer_q[l].astype(BF16), _peer_key_matrix(peer_sub_keys[l]))
        peer_out = _peer_dense(xn, e, gate, u_tab, v_tab, l)
        h_p, h_s = _ple(h1, peer_out, pp, ps, row(g_ple[l]), w_ple_gate[l].astype(BF16), row(b_ple_gate[l]),
                        w_ple[l].astype(BF16), row(g_final), l, final=(l == depth - 1))
    y_p, y_s = h_p, h_s
    k_p, v_p, k_s, v_s = kv

    def untranspose(a):
        return jnp.transpose(a.reshape(depth, b, N_HEADS, HEAD_DIM, t), (0, 1, 4, 2, 3))

    return (y_p.reshape(b, t, d), y_s.reshape(bs, ts, d), untranspose(k_p), untranspose(v_p),
            k_s.reshape(depth, bs, ts, N_HEADS, HEAD_DIM), v_s.reshape(depth, bs, ts, N_HEADS, HEAD_DIM),
            vas.reshape(depth, bs, ts, A_W // GROUP, GROUP))
```
